```python
import jax, jax.numpy as jnp
from jax import lax
import numpy as np

D_MODEL = 1024
BATCH = 16
SEQ = 4096
DEPTH = 2
DEC_BATCH = 8
DEC_SEQ = 2048
PAST_LEN = 128

HEAD_DIM = 64
RET_HEADS = 4
FFT_GROUPS = 4
ATT_Q_HEADS = 4
ATT_KV_HEADS = 2
CONV_GROUPS = 4
RET_W = RET_HEADS * HEAD_DIM
FFT_W = FFT_GROUPS * HEAD_DIM
ATT_W = ATT_Q_HEADS * HEAD_DIM
ATT_KV_W = ATT_KV_HEADS * HEAD_DIM
CONV_W = CONV_GROUPS * HEAD_DIM
MIX_W = RET_W + FFT_W + ATT_W + CONV_W
PROJ_SPLITS = (RET_W, RET_W, RET_W, RET_W, FFT_W, ATT_W, ATT_KV_W, ATT_KV_W, CONV_W, CONV_W, CONV_W)
PROJ_W = 4 * RET_W + FFT_W + ATT_W + 2 * ATT_KV_W + 3 * CONV_W
D_FF = 4 * D_MODEL
PLE_DIM = 256
GRID_W = 64
RET_CHUNK = 128
Q_BLOCK = 128
CONV_WIDTH = 3
ROPE_BASE = 10000.0
NORM_EPS = 1e-6

kernel_name = 'hybrid_parallel_heads_bidir_encoder'


def _rmsnorm(x, w):
    xf = x.astype(jnp.float32)
    y = xf * lax.rsqrt(jnp.mean(xf * xf, axis=-1, keepdims=True) + NORM_EPS)
    return (y * w.astype(jnp.float32)).astype(x.dtype)


def _rope(x, ang):
    m = ang.shape[-1]
    cos = jnp.cos(ang)[None, :, None, :]
    sin = jnp.sin(ang)[None, :, None, :]
    x1, x2 = x[..., :m], x[..., m:]
    return jnp.concatenate([x1 * cos - x2 * sin, x2 * cos + x1 * sin], axis=-1)


def _retention_dir(q, k, v, log_g, strict):
    b, h, s, d = q.shape
    nc = s // RET_CHUNK
    qc = q.reshape(b, h, nc, RET_CHUNK, d)
    kc = k.reshape(b, h, nc, RET_CHUNK, d)
    vc = v.reshape(b, h, nc, RET_CHUNK, d)
    idx = jnp.arange(RET_CHUNK, dtype=jnp.float32)
    diff = idx[:, None] - idx[None, :]
    mask = (diff > 0) if strict else (diff >= 0)
    decay = jnp.where(mask[None], jnp.exp(jnp.where(mask, diff, 0.0)[None] * log_g[:, None, None]), 0.0)
    scores = jnp.einsum('bhncd,bhnmd->bhncm', qc, kc) * decay[None, :, None]
    inner = jnp.einsum('bhncm,bhnmd->bhncd', scores, vc)
    zeta = jnp.exp((RET_CHUNK - 1 - idx)[None, :] * log_g[:, None])
    kv = jnp.einsum('bhncd,bhnce->nbhde', kc * zeta[None, :, None, :, None], vc)
    g_chunk = jnp.exp(RET_CHUNK * log_g)[None, :, None, None]

    def step(state, kv_n):
        return state * g_chunk + kv_n, state

    _, prev = lax.scan(step, jnp.zeros((b, h, d, d), jnp.float32), kv)
    xi = jnp.exp((idx + 1.0)[None, :] * log_g[:, None])
    cross = jnp.einsum('bhncd,nbhde->bhnce', qc, prev) * xi[None, :, None, :, None]
    return (inner + cross).reshape(b, h, s, d)


def _retention_mixer(q, k, v, g, log_rate, gn_w):
    b, s, _ = q.shape
    dt = q.dtype
    t = jnp.arange(s, dtype=jnp.float32)
    inv = ROPE_BASE ** (-jnp.arange(HEAD_DIM // 2, dtype=jnp.float32) / (HEAD_DIM // 2))
    ang = t[:, None] * inv[None, :]
    qh = _rope(q.astype(jnp.float32).reshape(b, s, RET_HEADS, HEAD_DIM), ang)
    kh = _rope(k.astype(jnp.float32).reshape(b, s, RET_HEADS, HEAD_DIM), ang) * (HEAD_DIM ** -0.5)
    vh = v.astype(jnp.float32).reshape(b, s, RET_HEADS, HEAD_DIM)
    qh, kh, vh = (a.transpose(0, 2, 1, 3) for a in (qh, kh, vh))
    log_g = -jnp.exp(log_rate.astype(jnp.float32))
    fwd = _retention_dir(qh, kh, vh, log_g[0], False)
    bwd = jnp.flip(_retention_dir(jnp.flip(qh, 2), jnp.flip(kh, 2), jnp.flip(vh, 2), log_g[1], True), 2)
    o = (fwd + bwd).transpose(0, 2, 1, 3)
    mu = jnp.mean(o, axis=-1, keepdims=True)
    var = jnp.mean(jnp.square(o - mu), axis=-1, keepdims=True)
    o = ((o - mu) * lax.rsqrt(var + NORM_EPS)).reshape(b, s, RET_W) * gn_w.astype(jnp.float32)
    return (jax.nn.silu(g.astype(jnp.float32)) * o).astype(dt)


def _fourier_mixer(u):
    b, s, _ = u.shape
    uf = u.astype(jnp.float32).reshape(b, s, FFT_GROUPS, HEAD_DIM)
    y = jnp.real(jnp.fft.fft2(uf, axes=(1, 3), norm='ortho'))
    return y.reshape(b, s, FFT_W).astype(u.dtype)


def _attention_mixer(q, k, v, q_norm_w, k_norm_w):
    b, s, _ = q.shape
    dt = q.dtype
    grp = ATT_Q_HEADS // ATT_KV_HEADS
    rows = s // GRID_W
    r_idx, c_idx = jnp.meshgrid(jnp.arange(rows, dtype=jnp.float32), jnp.arange(GRID_W, dtype=jnp.float32), indexing='ij')
    r_idx = r_idx.reshape(-1)
    c_idx = c_idx.reshape(-1)
    half = HEAD_DIM // 2
    inv = ROPE_BASE ** (-jnp.arange(half // 2, dtype=jnp.float32) / (half // 2))
    ang_r = r_idx[:, None] * inv[None, :]
    ang_c = c_idx[:, None] * inv[None, :]

    def axial(a):
        af = a.astype(jnp.float32)
        return jnp.concatenate([_rope(af[..., :half], ang_r), _rope(af[..., half:], ang_c)], axis=-1).astype(dt)

    qh = axial(_rmsnorm(q.reshape(b, s, ATT_Q_HEADS, HEAD_DIM), q_norm_w))
    kh = axial(_rmsnorm(k.reshape(b, s, ATT_KV_HEADS, HEAD_DIM), k_norm_w))
    vh = v.reshape(b, s, ATT_KV_HEADS, HEAD_DIM)
    qh = qh.reshape(b, s, ATT_KV_HEADS, grp, HEAD_DIM).transpose(0, 2, 3, 1, 4)
    kh = kh.transpose(0, 2, 1, 3)
    vh = vh.transpose(0, 2, 1, 3)
    nb = s // Q_BLOCK
    qb = jnp.moveaxis(qh.reshape(b, ATT_KV_HEADS, grp, nb, Q_BLOCK, HEAD_DIM), 3, 0)
    scale = HEAD_DIM ** -0.5

    def attend(qblk):
        sc = jnp.einsum('bkgqd,bksd->bkgqs', qblk, kh).astype(jnp.float32) * scale
        pr = jax.nn.softmax(sc, axis=-1)
        return jnp.einsum('bkgqs,bksd->bkgqd', pr.astype(vh.dtype), vh)

    o = lax.map(attend, qb)
    o = jnp.moveaxis(o, 0, 3).reshape(b, ATT_KV_HEADS, grp, s, HEAD_DIM)
    return o.transpose(0, 3, 1, 2, 4).reshape(b, s, ATT_W)


def _conv_mixer(b_gate, c_gate, h, conv_w):
    s = h.shape[1]
    pad = CONV_WIDTH // 2
    z = c_gate * h
    zp = jnp.pad(z, ((0, 0), (pad, pad), (0, 0)))
    y = zp[:, 0:s] * conv_w[0]
    for j in range(1, CONV_WIDTH):
        y = y + zp[:, j:j + s] * conv_w[j]
    return b_gate * y


def _encode(x, p, norm_mix_w, w_in, ret_log_rate, ret_gn_w, q_norm_w, k_norm_w, conv_w, w_out,
            norm_ffn_w, w_ffn1, w_ffn2, norm_pl_w, w_pl_gate, w_pl_proj, final_norm_w):
    offsets = np.cumsum(PROJ_SPLITS)[:-1].tolist()
    for l in range(DEPTH):
        h = _rmsnorm(x, norm_mix_w[l])
        proj = h @ w_in[l]
        rq, rk, rv, rg, fu, aq, ak, av, cb, cc, ch = jnp.split(proj, offsets, axis=-1)
        y_ret = _retention_mixer(rq, rk, rv, rg, ret_log_rate[l], ret_gn_w[l])
        y_fft = _fourier_mixer(fu)
        y_att = _attention_mixer(aq, ak, av, q_norm_w[l], k_norm_w[l])
        y_conv = _conv_mixer(cb, cc, ch, conv_w[l])
        mixed = jnp.concatenate([y_ret, y_fft, y_att, y_conv], axis=-1)
        x = x + mixed @ w_out[l]
        hf = _rmsnorm(x, norm_ffn_w[l]) @ w_ffn1[l]
        x = x + jnp.square(jax.nn.relu(hf)) @ w_ffn2[l]
        gate = jax.nn.sigmoid(_rmsnorm(x, norm_pl_w[l]) @ w_pl_gate[l])
        x = x + gate * (p[l] @ w_pl_proj[l])
    return _rmsnorm(x, final_norm_w)


def setup_inputs(seed: int = 0) -> dict:
    key = jax.random.key(seed)
    ks = jax.random.split(key, 20)
    f32 = jnp.float32
    nrm = lambda k, shape, scale: jax.random.normal(k, shape, f32) * scale
    base_rate = -(5.0 + jnp.arange(RET_HEADS, dtype=f32)) * jnp.log(2.0)
    return {
        'x_prompt': nrm(ks[0], (BATCH, SEQ, D_MODEL), 1.0),
        'x_sample': nrm(ks[1], (DEC_BATCH, DEC_SEQ, D_MODEL), 1.0),
        'p_prompt': nrm(ks[2], (DEPTH, BATCH, SEQ, PLE_DIM), 1.0),
        'p_sample': nrm(ks[3], (DEPTH, DEC_BATCH, DEC_SEQ, PLE_DIM), 1.0),
        'norm_mix_w': 1.0 + nrm(ks[4], (DEPTH, D_MODEL), 0.02),
        'w_in': nrm(ks[5], (DEPTH, D_MODEL, PROJ_W), D_MODEL ** -0.5),
        'ret_log_rate': base_rate[None, None, :] + nrm(ks[6], (DEPTH, 2, RET_HEADS), 0.1),
        'ret_gn_w': 1.0 + nrm(ks[7], (DEPTH, RET_W), 0.02),
        'q_norm_w': 1.0 + nrm(ks[8], (DEPTH, HEAD_DIM), 0.02),
        'k_norm_w': 1.0 + nrm(ks[9], (DEPTH, HEAD_DIM), 0.02),
        'conv_w': nrm(ks[10], (DEPTH, CONV_WIDTH, CONV_W), CONV_WIDTH ** -0.5),
        'w_out': nrm(ks[11], (DEPTH, MIX_W, D_MODEL), MIX_W ** -0.5),
        'norm_ffn_w': 1.0 + nrm(ks[12], (DEPTH, D_MODEL), 0.02),
        'w_ffn1': nrm(ks[13], (DEPTH, D_MODEL, D_FF), D_MODEL ** -0.5),
        'w_ffn2': nrm(ks[14], (DEPTH, D_FF, D_MODEL), D_FF ** -0.5),
        'norm_pl_w': 1.0 + nrm(ks[15], (DEPTH, D_MODEL), 0.02),
        'w_pl_gate': nrm(ks[16], (DEPTH, D_MODEL, D_MODEL), D_MODEL ** -0.5),
        'w_pl_proj': nrm(ks[17], (DEPTH, PLE_DIM, D_MODEL), PLE_DIM ** -0.5),
        'final_norm_w': 1.0 + nrm(ks[18], (D_MODEL,), 0.02),
    }


def reference(x_prompt, x_sample, p_prompt, p_sample, norm_mix_w, w_in, ret_log_rate, ret_gn_w,
              q_norm_w, k_norm_w, conv_w, w_out, norm_ffn_w, w_ffn1, w_ffn2, norm_pl_w, w_pl_gate,
              w_pl_proj, final_norm_w):
    y_prompt = _encode(x_prompt, p_prompt, norm_mix_w, w_in, ret_log_rate, ret_gn_w, q_norm_w, k_norm_w,
                       conv_w, w_out, norm_ffn_w, w_ffn1, w_ffn2, norm_pl_w, w_pl_gate, w_pl_proj, final_norm_w)
    y_sample = _encode(x_sample, p_sample, norm_mix_w, w_in, ret_log_rate, ret_gn_w, q_norm_w, k_norm_w,
                       conv_w, w_out, norm_ffn_w, w_ffn1, w_ffn2, norm_pl_w, w_pl_gate, w_pl_proj, final_norm_w)
    return (y_prompt, y_sample)
```

```python
import functools

import numpy as np
import jax
import jax.numpy as jnp
from jax import lax
from jax.experimental import pallas as pl
from jax.experimental.pallas import tpu as pltpu

D_MODEL = 1024
HEAD_DIM = 64
RET_HEADS = 4
ATT_Q_HEADS = 4
ATT_KV_HEADS = 2
MIXER_W = 256
ATT_KV_W = ATT_KV_HEADS * HEAD_DIM
PROJ_W = 2560
D_FF = 4 * D_MODEL
PLE_DIM = 256
GRID_W = 64
RET_CHUNK = 128
ROPE_BASE = 10000.0
NORM_EPS = 1e-6
FFT_INNER = 64

LANES = 128
VMEM_LIMIT = 56 * 1024 * 1024

F32 = jnp.float32
BF16 = jnp.bfloat16

_O_RQ, _O_RK, _O_RV, _O_RG, _O_FU, _O_AQ, _O_AK, _O_AV, _O_CB, _O_CC, _O_CH = (
    0, 256, 512, 768, 1024, 1280, 1536, 1664, 1792, 2048, 2304)


def _params(*sem):
    return pltpu.CompilerParams(dimension_semantics=sem, vmem_limit_bytes=VMEM_LIMIT)


def _const_spec(shape):
    nd = len(shape)
    return pl.BlockSpec(shape, lambda *_: (0,) * nd, pipeline_mode=pl.Buffered(1))


def _rms(x, w):
    ms = jnp.mean(x * x, axis=-1, keepdims=True)
    return x * lax.rsqrt(ms + NORM_EPS) * w


def _pair_swap(v, half):
    parts = []
    for s in range(v.shape[-1] // LANES):
        vs = v[:, s * LANES:(s + 1) * LANES]
        lane = lax.broadcasted_iota(jnp.int32, vs.shape, 1)
        lo = (lane & half) == 0
        parts.append(jnp.where(lo, pltpu.roll(vs, LANES - half, 1), pltpu.roll(vs, half, 1)))
    return parts[0] if len(parts) == 1 else jnp.concatenate(parts, axis=-1)


def _dup_heads(v):
    sw = pltpu.roll(v, HEAD_DIM, 1)
    lane = lax.broadcasted_iota(jnp.int32, v.shape, 1)
    lo = lane < HEAD_DIM
    return jnp.concatenate([jnp.where(lo, v, sw), jnp.where(lo, sw, v)], axis=-1)


def _in_kernel(x_ref, nw_ref, w_ref, rcos_ref, rsin_ref, acos_ref, asin_ref, qw_ref, kw_ref, g_ref,
               rq_ref, rkt_ref, rv_ref, rg_ref, fu_ref, aq_ref, akt_ref, av_ref, cb_ref, z_ref):
    h = _rms(x_ref[0], nw_ref[...]).astype(BF16)

    def seg(off, width=MIXER_W):
        return jnp.dot(h, w_ref[:, off:off + width], preferred_element_type=F32)

    rcos, rsin = rcos_ref[...], rsin_ref[...]
    q = seg(_O_RQ)
    rq_ref[0] = (q * rcos + _pair_swap(q, HEAD_DIM // 2) * rsin).astype(BF16)
    k = seg(_O_RK)
    k = (k * rcos + _pair_swap(k, HEAD_DIM // 2) * rsin) * (HEAD_DIM ** -0.5)
    rkt_ref[0] = k.T.astype(BF16)
    rv_ref[0] = seg(_O_RV).astype(BF16)
    g = seg(_O_RG)
    rg_ref[0] = (g * jax.nn.sigmoid(g)).astype(BF16)
    fu_ref[0] = seg(_O_FU).astype(BF16)

    acos, asin = acos_ref[...], asin_ref[...]
    gm = g_ref[...]
    q = seg(_O_AQ)
    ms = jnp.dot((q * q).astype(BF16), gm, preferred_element_type=F32)
    q = q * lax.rsqrt(ms + NORM_EPS) * qw_ref[...]
    q = (q * acos + _pair_swap(q, HEAD_DIM // 4) * asin) * (HEAD_DIM ** -0.5)
    aq_ref[0] = q.astype(BF16)
    k = seg(_O_AK, ATT_KV_W)
    ms = jnp.dot((k * k).astype(BF16), gm[:ATT_KV_W, :ATT_KV_W], preferred_element_type=F32)
    k = k * lax.rsqrt(ms + NORM_EPS) * kw_ref[...]
    k = k * acos[:, :ATT_KV_W] + _pair_swap(k, HEAD_DIM // 4) * asin[:, :ATT_KV_W]
    akt_ref[0] = _dup_heads(k).T.astype(BF16)
    av_ref[0] = _dup_heads(seg(_O_AV, ATT_KV_W)).astype(BF16)

    cb_ref[0] = seg(_O_CB).astype(BF16)
    z_ref[0] = (seg(_O_CC) * seg(_O_CH)).astype(BF16)


def _in_proj(x, nw, w_in, tabs, qw, kw, gmean, tm):
    b, s, _ = x.shape
    rcos, rsin, acos, asin = tabs
    tok = pl.BlockSpec((1, tm, MIXER_W), lambda i, j: (i, j, 0))
    tok_t = pl.BlockSpec((1, MIXER_W, tm), lambda i, j: (i, 0, j))
    tab = pl.BlockSpec((tm, MIXER_W), lambda i, j: (j, 0))
    nat = jax.ShapeDtypeStruct((b, s, MIXER_W), BF16)
    tr = jax.ShapeDtypeStruct((b, MIXER_W, s), BF16)
    return pl.pallas_call(
        _in_kernel,
        grid=(b, s // tm),
        in_specs=[pl.BlockSpec((1, tm, D_MODEL), lambda i, j: (i, j, 0)),
                  _const_spec((1, D_MODEL)), _const_spec((D_MODEL, PROJ_W)),
                  tab, tab, tab, tab,
                  _const_spec((1, MIXER_W)), _const_spec((1, ATT_KV_W)), _const_spec((MIXER_W, MIXER_W))],
        out_specs=[tok, tok_t, tok, tok, tok, tok, tok_t, tok, tok, tok],
        out_shape=[nat, tr, nat, nat, nat, nat, tr, nat, nat, nat],
        compiler_params=_params("parallel", "parallel"),
    )(x, nw, w_in, rcos, rsin, acos, asin, qw, kw, gmean)


def _ret_kernel(q_ref, kt_ref, v_ref, g_ref, dec_ref, zf_ref, zb_ref, xf_ref, xb_ref, gf_ref, gb_ref,
                bd_ref, gm_ref, gnw_ref, o_ref, acc_ref, sf_ref, sb_ref):
    c = RET_CHUNK
    nc = q_ref.shape[1] // c
    head = lax.broadcasted_iota(jnp.int32, (c, MIXER_W), 1) // HEAD_DIM

    def chunk(n):
        r = pl.ds(pl.multiple_of(n * c, c), c)
        return q_ref[0, r, :], kt_ref[0, :, r], v_ref[0, r, :], r

    sf_ref[...] = jnp.zeros_like(sf_ref)
    sb_ref[...] = jnp.zeros_like(sb_ref)

    def fwd(n, carry):
        qc, ktc, vc, r = chunk(n)
        qs = jnp.concatenate([jnp.where(head == h, qc, jnp.zeros_like(qc)) for h in range(RET_HEADS)], axis=0)
        sc = jnp.dot(qs, ktc, preferred_element_type=F32) * dec_ref[...]
        inner = jnp.dot(sc.astype(BF16), vc, preferred_element_type=F32)
        acc = jnp.dot(qc, sf_ref[...].astype(BF16), preferred_element_type=F32) * xf_ref[...]
        for h in range(RET_HEADS):
            acc = acc + jnp.where(head == h, inner[h * c:(h + 1) * c], 0.0)
        acc_ref[r, :] = acc
        kz = (ktc.astype(F32) * zf_ref[...]).astype(BF16)
        sf_ref[...] = sf_ref[...] * gf_ref[...] + jnp.dot(kz, vc, preferred_element_type=F32) * bd_ref[...]
        return carry

    lax.fori_loop(0, nc, fwd, 0)

    def bwd(i, carry):
        n = nc - 1 - i
        qc, ktc, vc, r = chunk(n)
        o = acc_ref[r, :] + jnp.dot(qc, sb_ref[...].astype(BF16), preferred_element_type=F32) * xb_ref[...]
        kz = (ktc.astype(F32) * zb_ref[...]).astype(BF16)
        sb_ref[...] = sb_ref[...] * gb_ref[...] + jnp.dot(kz, vc, preferred_element_type=F32) * bd_ref[...]
        gm = gm_ref[...]
        hi = o.astype(BF16)
        lo = (o - hi.astype(F32)).astype(BF16)
        mu = jnp.dot(hi, gm, preferred_element_type=F32) + jnp.dot(lo, gm, preferred_element_type=F32)
        d = o - mu
        var = jnp.dot((d * d).astype(BF16), gm, preferred_element_type=F32)
        y = d * lax.rsqrt(var + NORM_EPS) * gnw_ref[...]
        o_ref[0, r, :] = (g_ref[0, r, :].astype(F32) * y).astype(BF16)
        return carry

    lax.fori_loop(0, nc, bwd, 0)


def _retention(rq, rkt, rv, rg, tabs, gmean, gnw):
    b, s, _ = rq.shape
    nat = pl.BlockSpec((1, s, MIXER_W), lambda i: (i, 0, 0))
    tr = pl.BlockSpec((1, MIXER_W, s), lambda i: (i, 0, 0))
    return pl.pallas_call(
        _ret_kernel,
        grid=(b,),
        in_specs=[nat, tr, nat, nat] + [_const_spec(t.shape) for t in tabs]
                 + [_const_spec((MIXER_W, MIXER_W)), _const_spec((1, MIXER_W))],
        out_specs=nat,
        out_shape=jax.ShapeDtypeStruct((b, s, MIXER_W), BF16),
        scratch_shapes=[pltpu.VMEM((s, MIXER_W), F32), pltpu.VMEM((MIXER_W, MIXER_W), F32),
                        pltpu.VMEM((MIXER_W, MIXER_W), F32)],
        compiler_params=_params("parallel"),
    )(rq, rkt, rv, rg, *tabs, gmean, gnw)


def _retention_tables(log_rate):
    c = RET_CHUNK
    log_g = -jnp.exp(log_rate.astype(F32))
    lf, lb = log_g[0], log_g[1]
    idx = jnp.arange(c, dtype=F32)
    diff = idx[:, None] - idx[None, :]
    low = diff >= 0
    up = diff < 0
    dec_f = jnp.where(low[None], jnp.exp(jnp.where(low, diff, 0.0)[None] * lf[:, None, None]), 0.0)
    dec_b = jnp.where(up[None], jnp.exp(jnp.where(up, -diff, 0.0)[None] * lb[:, None, None]), 0.0)
    dec = (dec_f + dec_b).reshape(RET_HEADS * c, c)
    lane = lambda per_head: jnp.repeat(per_head, HEAD_DIM, axis=-1)
    zeta_f = lane(jnp.exp((c - 1 - idx)[:, None] * lf[None, :])).T
    zeta_b = lane(jnp.exp(idx[:, None] * lb[None, :])).T
    xi_f = lane(jnp.exp((idx + 1.0)[:, None] * lf[None, :]))
    xi_b = lane(jnp.exp((c - idx)[:, None] * lb[None, :]))
    hd = jnp.arange(MIXER_W) // HEAD_DIM
    bd = (hd[:, None] == hd[None, :]).astype(F32)
    g_f = bd * lane(jnp.exp(c * lf)[None, :])
    g_b = bd * lane(jnp.exp(c * lb)[None, :])
    return dec, zeta_f, zeta_b, xi_f, xi_b, g_f, g_b, bd


def _fft1_kernel(x_ref, f_ref, tc_ref, ts_ref, o_ref):
    n1 = x_ref.shape[1]
    a = jnp.dot(f_ref[...], x_ref[0], preferred_element_type=F32)
    ar, ai = a[:n1], a[n1:]
    tc, ts = tc_ref[...], ts_ref[...]
    o_ref[0, 0] = (ar * tc + ai * ts).astype(BF16)
    o_ref[0, 1] = (ai * tc - ar * ts).astype(BF16)


def _fft2_kernel(a_ref, g_ref, c_ref, s_ref, o_ref, *, scale):
    k = a_ref.shape[2]
    n2 = a_ref.shape[3]
    for j in range(k):
        r = jnp.concatenate([a_ref[0, 0, j], a_ref[0, 1, j]], axis=0)
        z = jnp.dot(g_ref[...], r, preferred_element_type=F32)
        y = (jnp.dot(z[:n2].astype(BF16), c_ref[...], preferred_element_type=F32)
             + jnp.dot(z[n2:].astype(BF16), s_ref[...], preferred_element_type=F32))
        o_ref[0, :, j * MIXER_W:(j + 1) * MIXER_W] = (y * scale).astype(BF16)


def _dft_cs(n):
    j = np.arange(n)
    ang = 2.0 * np.pi * ((j[:, None] * j[None, :]) % n) / n
    return np.cos(ang), np.sin(ang)


def _fourier(fu, col_block, slabs):
    b, s, w = fu.shape
    n2 = FFT_INNER
    n1 = s // n2
    c1, s1 = _dft_cs(n1)
    f1 = jnp.asarray(np.concatenate([c1, -s1], axis=0), F32).astype(BF16)
    ang = 2.0 * np.pi * (np.arange(n1)[:, None] * np.arange(n2)[None, :]) / s
    tw_c = jnp.repeat(jnp.asarray(np.cos(ang), F32), w, axis=1)
    tw_s = jnp.repeat(jnp.asarray(np.sin(ang), F32), w, axis=1)
    c2, s2 = _dft_cs(n2)
    g2 = jnp.asarray(np.block([[c2, s2], [-s2, c2]]), F32).astype(BF16)
    cc, sc = _dft_cs(HEAD_DIM)
    eye = np.eye(w // HEAD_DIM)
    cbd = jnp.asarray(np.kron(eye, cc), F32).astype(BF16)
    sbd = jnp.asarray(np.kron(eye, sc), F32).astype(BF16)

    cols = n2 * w
    x = fu.reshape(b, n1, cols)
    a = pl.pallas_call(
        _fft1_kernel,
        grid=(cols // col_block, b),
        in_specs=[pl.BlockSpec((1, n1, col_block), lambda j, i: (i, 0, j)),
                  _const_spec((2 * n1, n1)),
                  pl.BlockSpec((n1, col_block), lambda j, i: (0, j)),
                  pl.BlockSpec((n1, col_block), lambda j, i: (0, j))],
        out_specs=pl.BlockSpec((1, 2, n1, col_block), lambda j, i: (i, 0, 0, j)),
        out_shape=jax.ShapeDtypeStruct((b, 2, n1, cols), BF16),
        compiler_params=_params("parallel", "parallel"),
    )(x, f1, tw_c, tw_s)
    a = a.reshape(b, 2, n1, n2, w)
    y = pl.pallas_call(
        functools.partial(_fft2_kernel, scale=float((s * HEAD_DIM) ** -0.5)),
        grid=(b, n1 // slabs),
        in_specs=[pl.BlockSpec((1, 2, slabs, n2, w), lambda i, j: (i, 0, j, 0, 0)),
                  _const_spec((2 * n2, 2 * n2)), _const_spec((w, w)), _const_spec((w, w))],
        out_specs=pl.BlockSpec((1, n2, slabs * w), lambda i, j: (i, 0, j)),
        out_shape=jax.ShapeDtypeStruct((b, n2, n1 * w), BF16),
        compiler_params=_params("parallel", "parallel"),
    )(a, g2, cbd, sbd)
    return y.reshape(b, s, w)


def _att_kernel(q_ref, kt_ref, v_ref, o_ref):
    q = q_ref[0]
    tq = q.shape[0]
    head = lax.broadcasted_iota(jnp.int32, q.shape, 1) // HEAD_DIM
    qs = jnp.concatenate([jnp.where(head == h, q, jnp.zeros_like(q)) for h in range(ATT_Q_HEADS)], axis=0)
    sc = jnp.dot(qs, kt_ref[0], preferred_element_type=F32)
    p = jnp.exp(sc - jnp.max(sc, axis=-1, keepdims=True))
    l = jnp.sum(p, axis=-1, keepdims=True)
    pv = jnp.dot(p.astype(BF16), v_ref[0], preferred_element_type=F32) / l
    out = jnp.where(head == 0, pv[:tq], 0.0)
    for h in range(1, ATT_Q_HEADS):
        out = out + jnp.where(head == h, pv[h * tq:(h + 1) * tq], 0.0)
    o_ref[0] = out.astype(BF16)


def _attention(aq, akt, av, tq):
    b, s, w = aq.shape
    return pl.pallas_call(
        _att_kernel,
        grid=(b, s // tq),
        in_specs=[pl.BlockSpec((1, tq, w), lambda i, j: (i, j, 0)),
                  pl.BlockSpec((1, w, s), lambda i, j: (i, 0, 0)),
                  pl.BlockSpec((1, s, w), lambda i, j: (i, 0, 0))],
        out_specs=pl.BlockSpec((1, tq, w), lambda i, j: (i, j, 0)),
        out_shape=jax.ShapeDtypeStruct((b, s, w), BF16),
        compiler_params=_params("parallel", "parallel"),
    )(aq, akt, av)


def _post_kernel(x_ref, ret_ref, fft_ref, att_ref, cb_ref, z_ref, halo_ref, cw_ref, wo_ref,
                 nf_ref, w1_ref, w2_ref, np_ref, wg_ref, p_ref, wp_ref, fn_ref, o_ref, *, final, ff_block):
    tm = x_ref.shape[1]
    z = z_ref[0].astype(F32)
    row = lax.broadcasted_iota(jnp.int32, z.shape, 0)
    halo = halo_ref[0, 0].astype(F32)
    z_prev = jnp.where(row == 0, halo[0:1], pltpu.roll(z, 1, 0))
    z_next = jnp.where(row == tm - 1, halo[1:2], pltpu.roll(z, tm - 1, 0))
    cw = cw_ref[...]
    conv = cb_ref[0].astype(F32) * (z_prev * cw[0:1] + z * cw[1:2] + z_next * cw[2:3])
    mixed = jnp.concatenate([ret_ref[0], fft_ref[0], att_ref[0], conv.astype(BF16)], axis=-1)
    x = x_ref[0] + jnp.dot(mixed, wo_ref[...], preferred_element_type=F32)

    hn = _rms(x, nf_ref[...]).astype(BF16)
    acc = x
    for j in range(D_FF // ff_block):
        hf = jnp.dot(hn, w1_ref[:, j * ff_block:(j + 1) * ff_block], preferred_element_type=F32)
        act = jnp.square(jnp.maximum(hf, 0.0)).astype(BF16)
        acc = acc + jnp.dot(act, w2_ref[j * ff_block:(j + 1) * ff_block, :], preferred_element_type=F32)
    x = acc

    gate = jax.nn.sigmoid(jnp.dot(_rms(x, np_ref[...]).astype(BF16), wg_ref[...], preferred_element_type=F32))
    x = x + gate * jnp.dot(p_ref[0].astype(BF16), wp_ref[...], preferred_element_type=F32)
    if final:
        x = _rms(x, fn_ref[...])
    o_ref[0] = x


def _post(x, y_ret, y_fft, y_att, cb, z, conv_w, w_out, nf, w1, w2, npl, wg, p, wp, fn, tm, final):
    b, s, _ = x.shape
    nt = s // tm
    zero = jnp.zeros((b, 1, MIXER_W), z.dtype)
    prev = jnp.concatenate([zero, z[:, tm - 1::tm][:, :-1]], axis=1)
    nxt = jnp.concatenate([z[:, ::tm][:, 1:], zero], axis=1)
    halo = jnp.concatenate([prev[:, :, None], nxt[:, :, None],
                            jnp.zeros((b, nt, 6, MIXER_W), z.dtype)], axis=2)
    tok = pl.BlockSpec((1, tm, MIXER_W), lambda i, j: (i, j, 0))
    xs = pl.BlockSpec((1, tm, D_MODEL), lambda i, j: (i, j, 0))
    return pl.pallas_call(
        functools.partial(_post_kernel, final=final, ff_block=1024),
        grid=(b, nt),
        in_specs=[xs, tok, tok, tok, tok, tok,
                  pl.BlockSpec((1, 1, 8, MIXER_W), lambda i, j: (i, j, 0, 0)),
                  _const_spec((3, MIXER_W)), _const_spec((D_MODEL, D_MODEL)),
                  _const_spec((1, D_MODEL)), _const_spec((D_MODEL, D_FF)), _const_spec((D_FF, D_MODEL)),
                  _const_spec((1, D_MODEL)), _const_spec((D_MODEL, D_MODEL)),
                  pl.BlockSpec((1, tm, PLE_DIM), lambda i, j: (i, j, 0)),
                  _const_spec((PLE_DIM, D_MODEL)), _const_spec((1, D_MODEL))],
        out_specs=xs,
        out_shape=jax.ShapeDtypeStruct((b, s, D_MODEL), F32),
        compiler_params=_params("parallel", "parallel"),
    )(x, y_ret, y_fft, y_att, cb, z, halo, conv_w, w_out, nf, w1, w2, npl, wg, p, wp, fn)


def _rope_tables(s):
    t = jnp.arange(s, dtype=F32)
    half = HEAD_DIM // 2
    inv = ROPE_BASE ** (-jnp.arange(half, dtype=F32) / half)
    ang = t[:, None] * inv[None, :]
    cos, sin = jnp.cos(ang), jnp.sin(ang)
    rcos = jnp.tile(jnp.concatenate([cos, cos], axis=-1), (1, RET_HEADS))
    rsin = jnp.tile(jnp.concatenate([-sin, sin], axis=-1), (1, RET_HEADS))
    quarter = half // 2
    inv_a = ROPE_BASE ** (-jnp.arange(quarter, dtype=F32) / quarter)
    ang_r = (jnp.arange(s) // GRID_W).astype(F32)[:, None] * inv_a[None, :]
    ang_c = (jnp.arange(s) % GRID_W).astype(F32)[:, None] * inv_a[None, :]
    cr, sr, cc, sc = jnp.cos(ang_r), jnp.sin(ang_r), jnp.cos(ang_c), jnp.sin(ang_c)
    acos = jnp.tile(jnp.concatenate([cr, cr, cc, cc], axis=-1), (1, ATT_Q_HEADS))
    asin = jnp.tile(jnp.concatenate([-sr, sr, -sc, sc], axis=-1), (1, ATT_Q_HEADS))
    return rcos, rsin, acos, asin


def _encode(x, p, wts, depth):
    b, s, _ = x.shape
    tabs = _rope_tables(s)
    hd = np.arange(MIXER_W) // HEAD_DIM
    gmean = jnp.asarray((hd[:, None] == hd[None, :]) / HEAD_DIM, F32).astype(BF16)
    for l in range(depth):
        w = wts[l]
        rq, rkt, rv, rg, fu, aq, akt, av, cb, z = _in_proj(
            x, w["norm_mix"], w["w_in"], tabs, w["q_norm"], w["k_norm"], gmean, tm=512)
        y_ret = _retention(rq, rkt, rv, rg, w["ret_tabs"], gmean, w["ret_gn"])
        y_fft = _fourier(fu, col_block=4096, slabs=8)
        y_att = _attention(aq, akt, av, tq=128)
        x = _post(x, y_ret, y_fft, y_att, cb, z, w["conv_w"], w["w_out"], w["norm_ffn"], w["w_ffn1"],
                  w["w_ffn2"], w["norm_pl"], w["w_pl_gate"], p[l], w["w_pl_proj"], w["final_norm"],
                  tm=512, final=(l == depth - 1))
    return x


def kernel(x_prompt, x_sample, p_prompt, p_sample, norm_mix_w, w_in, ret_log_rate, ret_gn_w, q_norm_w, k_norm_w,
           conv_w, w_out, norm_ffn_w, w_ffn1, w_ffn2, norm_pl_w, w_pl_gate, w_pl_proj, final_norm_w):
    depth = w_in.shape[0]
    wts = []
    for l in range(depth):
        wts.append(dict(
            norm_mix=norm_mix_w[l][None, :], w_in=w_in[l].astype(BF16),
            ret_tabs=_retention_tables(ret_log_rate[l]), ret_gn=ret_gn_w[l][None, :],
            q_norm=jnp.tile(q_norm_w[l], ATT_Q_HEADS)[None, :], k_norm=jnp.tile(k_norm_w[l], ATT_KV_HEADS)[None, :],
            conv_w=conv_w[l], w_out=w_out[l].astype(BF16), norm_ffn=norm_ffn_w[l][None, :],
            w_ffn1=w_ffn1[l].astype(BF16), w_ffn2=w_ffn2[l].astype(BF16), norm_pl=norm_pl_w[l][None, :],
            w_pl_gate=w_pl_gate[l].astype(BF16), w_pl_proj=w_pl_proj[l].astype(BF16),
            final_norm=final_norm_w[None, :]))
    return _encode(x_prompt, p_prompt, wts, depth), _encode(x_sample, p_sample, wts, depth)
```

```python
import functools

import numpy as np
import jax
import jax.numpy as jnp
from jax import lax
from jax.experimental import pallas as pl
from jax.experimental.pallas import tpu as pltpu

D_MODEL = 1024
HEAD_DIM = 64
RET_HEADS = 4
ATT_Q_HEADS = 4
ATT_KV_HEADS = 2
MIXER_W = 256
ATT_KV_W = ATT_KV_HEADS * HEAD_DIM
PROJ_W = 2560
D_FF = 4 * D_MODEL
PLE_DIM = 256
GRID_W = 64
RET_CHUNK = 128
ROPE_BASE = 10000.0
NORM_EPS = 1e-6
LOG2_E = 1.4426950408889634
FFT_INNER = 64

LANES = 128
VMEM_LIMIT = 56 * 1024 * 1024

F32 = jnp.float32
BF16 = jnp.bfloat16

_O_RQ, _O_RK, _O_RV, _O_RG, _O_FU, _O_AQ, _O_AK, _O_AV, _O_CB, _O_CC, _O_CH = (
    0, 256, 512, 768, 1024, 1280, 1536, 1664, 1792, 2048, 2304)


def _params(*sem):
    return pltpu.CompilerParams(dimension_semantics=sem, vmem_limit_bytes=VMEM_LIMIT)


def _const_spec(shape):
    nd = len(shape)
    return pl.BlockSpec(shape, lambda *_: (0,) * nd, pipeline_mode=pl.Buffered(1))


def _rms(x, w):
    ms = jnp.mean(x * x, axis=-1, keepdims=True)
    return x * lax.rsqrt(ms + NORM_EPS) * w


def _pair_swap(v, half):
    parts = []
    for s in range(v.shape[-1] // LANES):
        vs = v[:, s * LANES:(s + 1) * LANES]
        lane = lax.broadcasted_iota(jnp.int32, vs.shape, 1)
        lo = (lane & half) == 0
        parts.append(jnp.where(lo, pltpu.roll(vs, LANES - half, 1), pltpu.roll(vs, half, 1)))
    return parts[0] if len(parts) == 1 else jnp.concatenate(parts, axis=-1)


def _dup_heads(v):
    sw = pltpu.roll(v, HEAD_DIM, 1)
    lane = lax.broadcasted_iota(jnp.int32, v.shape, 1)
    lo = lane < HEAD_DIM
    return jnp.concatenate([jnp.where(lo, v, sw), jnp.where(lo, sw, v)], axis=-1)


def _in_kernel(x_ref, nw_ref, w_ref, rcos_ref, rsin_ref, acos_ref, asin_ref, qw_ref, kw_ref, g_ref,
               rq_ref, rkt_ref, rv_ref, rg_ref, fu_ref, aq_ref, akt_ref, av_ref, cb_ref, z_ref):
    h = _rms(x_ref[0], nw_ref[...]).astype(BF16)

    def seg(off, width=MIXER_W):
        return jnp.dot(h, w_ref[:, off:off + width], preferred_element_type=F32)

    rcos, rsin = rcos_ref[...], rsin_ref[...]
    q = seg(_O_RQ)
    rq_ref[0] = (q * rcos + _pair_swap(q, HEAD_DIM // 2) * rsin).astype(BF16)
    k = seg(_O_RK)
    k = (k * rcos + _pair_swap(k, HEAD_DIM // 2) * rsin) * (HEAD_DIM ** -0.5)
    rkt_ref[0] = k.T.astype(BF16)
    rv_ref[0] = seg(_O_RV).astype(BF16)
    g = seg(_O_RG)
    rg_ref[0] = (g * jax.nn.sigmoid(g)).astype(BF16)
    fu_ref[0] = seg(_O_FU).astype(BF16)

    acos, asin = acos_ref[...], asin_ref[...]
    gm = g_ref[...]
    q = seg(_O_AQ)
    ms = jnp.dot((q * q).astype(BF16), gm, preferred_element_type=F32)
    q = q * lax.rsqrt(ms + NORM_EPS) * qw_ref[...]
    q = (q * acos + _pair_swap(q, HEAD_DIM // 4) * asin) * (HEAD_DIM ** -0.5 * LOG2_E)
    aq_ref[0] = q.astype(BF16)
    k = seg(_O_AK, ATT_KV_W)
    ms = jnp.dot((k * k).astype(BF16), gm[:ATT_KV_W, :ATT_KV_W], preferred_element_type=F32)
    k = k * lax.rsqrt(ms + NORM_EPS) * kw_ref[...]
    k = k * acos[:, :ATT_KV_W] + _pair_swap(k, HEAD_DIM // 4) * asin[:, :ATT_KV_W]
    akt_ref[0] = _dup_heads(k).T.astype(BF16)
    v = seg(_O_AV, ATT_KV_W)
    av_ref[0] = jnp.concatenate([v, jnp.ones_like(v)], axis=-1).astype(BF16)

    cb_ref[0] = seg(_O_CB).astype(BF16)
    z_ref[0] = (seg(_O_CC) * seg(_O_CH)).astype(BF16)


def _in_proj(x, nw, w_in, tabs, qw, kw, gmean, tm):
    b, s, _ = x.shape
    rcos, rsin, acos, asin = tabs
    tok = pl.BlockSpec((1, tm, MIXER_W), lambda i, j: (i, j, 0))
    tok_t = pl.BlockSpec((1, MIXER_W, tm), lambda i, j: (i, 0, j))
    tab = pl.BlockSpec((tm, MIXER_W), lambda i, j: (j, 0))
    nat = jax.ShapeDtypeStruct((b, s, MIXER_W), BF16)
    tr = jax.ShapeDtypeStruct((b, MIXER_W, s), BF16)
    return pl.pallas_call(
        _in_kernel,
        grid=(b, s // tm),
        in_specs=[pl.BlockSpec((1, tm, D_MODEL), lambda i, j: (i, j, 0)),
                  _const_spec((1, D_MODEL)), _const_spec((D_MODEL, PROJ_W)),
                  tab, tab, tab, tab,
                  _const_spec((1, MIXER_W)), _const_spec((1, ATT_KV_W)), _const_spec((MIXER_W, MIXER_W))],
        out_specs=[tok, tok_t, tok, tok, tok, tok, tok_t, tok, tok, tok],
        out_shape=[nat, tr, nat, nat, nat, nat, tr, nat, nat, nat],
        compiler_params=_params("parallel", "parallel"),
    )(x, nw, w_in, rcos, rsin, acos, asin, qw, kw, gmean)


def _ret_kernel(q_ref, kt_ref, v_ref, g_ref, dec_ref, zf_ref, zb_ref, xf_ref, xb_ref, gf_ref, gb_ref,
                bd_ref, gm_ref, gnw_ref, o_ref, acc_ref, sf_ref, sb_ref):
    c = RET_CHUNK
    nc = q_ref.shape[1] // c
    head = lax.broadcasted_iota(jnp.int32, (c, MIXER_W), 1) // HEAD_DIM

    def chunk(n):
        r = pl.ds(pl.multiple_of(n * c, c), c)
        return q_ref[0, r, :], kt_ref[0, :, r], v_ref[0, r, :], r

    sf_ref[...] = jnp.zeros_like(sf_ref)
    sb_ref[...] = jnp.zeros_like(sb_ref)

    def fwd(n, carry):
        qc, ktc, vc, r = chunk(n)
        qs = jnp.concatenate([jnp.where(head == h, qc, jnp.zeros_like(qc)) for h in range(RET_HEADS)], axis=0)
        sc = jnp.dot(qs, ktc, preferred_element_type=F32) * dec_ref[...]
        inner = jnp.dot(sc.astype(BF16), vc, preferred_element_type=F32)
        acc = jnp.dot(qc, sf_ref[...].astype(BF16), preferred_element_type=F32) * xf_ref[...]
        for h in range(RET_HEADS):
            acc = acc + jnp.where(head == h, inner[h * c:(h + 1) * c], 0.0)
        acc_ref[r, :] = acc
        kz = (ktc.astype(F32) * zf_ref[...]).astype(BF16)
        sf_ref[...] = sf_ref[...] * gf_ref[...] + jnp.dot(kz, vc, preferred_element_type=F32) * bd_ref[...]
        return carry

    lax.fori_loop(0, nc, fwd, 0)

    def bwd(i, carry):
        n = nc - 1 - i
        qc, ktc, vc, r = chunk(n)
        o = acc_ref[r, :] + jnp.dot(qc, sb_ref[...].astype(BF16), preferred_element_type=F32) * xb_ref[...]
        kz = (ktc.astype(F32) * zb_ref[...]).astype(BF16)
        sb_ref[...] = sb_ref[...] * gb_ref[...] + jnp.dot(kz, vc, preferred_element_type=F32) * bd_ref[...]
        gm = gm_ref[...]
        hi = o.astype(BF16)
        lo = (o - hi.astype(F32)).astype(BF16)
        mu = jnp.dot(hi, gm, preferred_element_type=F32) + jnp.dot(lo, gm, preferred_element_type=F32)
        d = o - mu
        var = jnp.dot((d * d).astype(BF16), gm, preferred_element_type=F32)
        y = d * lax.rsqrt(var + NORM_EPS) * gnw_ref[...]
        o_ref[0, r, :] = (g_ref[0, r, :].astype(F32) * y).astype(BF16)
        return carry

    lax.fori_loop(0, nc, bwd, 0)


def _retention(rq, rkt, rv, rg, tabs, gmean, gnw):
    b, s, _ = rq.shape
    nat = pl.BlockSpec((1, s, MIXER_W), lambda i: (i, 0, 0))
    tr = pl.BlockSpec((1, MIXER_W, s), lambda i: (i, 0, 0))
    return pl.pallas_call(
        _ret_kernel,
        grid=(b,),
        in_specs=[nat, tr, nat, nat] + [_const_spec(t.shape) for t in tabs]
                 + [_const_spec((MIXER_W, MIXER_W)), _const_spec((1, MIXER_W))],
        out_specs=nat,
        out_shape=jax.ShapeDtypeStruct((b, s, MIXER_W), BF16),
        scratch_shapes=[pltpu.VMEM((s, MIXER_W), F32), pltpu.VMEM((MIXER_W, MIXER_W), F32),
                        pltpu.VMEM((MIXER_W, MIXER_W), F32)],
        compiler_params=_params("parallel"),
    )(rq, rkt, rv, rg, *tabs, gmean, gnw)


def _retention_tables(log_rate):
    c = RET_CHUNK
    log_g = -jnp.exp(log_rate.astype(F32))
    lf, lb = log_g[0], log_g[1]
    idx = jnp.arange(c, dtype=F32)
    diff = idx[:, None] - idx[None, :]
    low = diff >= 0
    up = diff < 0
    dec_f = jnp.where(low[None], jnp.exp(jnp.where(low, diff, 0.0)[None] * lf[:, None, None]), 0.0)
    dec_b = jnp.where(up[None], jnp.exp(jnp.where(up, -diff, 0.0)[None] * lb[:, None, None]), 0.0)
    dec = (dec_f + dec_b).reshape(RET_HEADS * c, c)
    lane = lambda per_head: jnp.repeat(per_head, HEAD_DIM, axis=-1)
    zeta_f = lane(jnp.exp((c - 1 - idx)[:, None] * lf[None, :])).T
    zeta_b = lane(jnp.exp(idx[:, None] * lb[None, :])).T
    xi_f = lane(jnp.exp((idx + 1.0)[:, None] * lf[None, :]))
    xi_b = lane(jnp.exp((c - idx)[:, None] * lb[None, :]))
    hd = jnp.arange(MIXER_W) // HEAD_DIM
    bd = (hd[:, None] == hd[None, :]).astype(F32)
    g_f = bd * lane(jnp.exp(c * lf)[None, :])
    g_b = bd * lane(jnp.exp(c * lb)[None, :])
    return dec, zeta_f, zeta_b, xi_f, xi_b, g_f, g_b, bd


def _fft1_kernel(x_ref, f_ref, tc_ref, ts_ref, o_ref):
    n1 = x_ref.shape[1]
    a = jnp.dot(f_ref[...], x_ref[0], preferred_element_type=F32)
    ar, ai = a[:n1], a[n1:]
    tc, ts = tc_ref[...], ts_ref[...]
    o_ref[0, 0] = (ar * tc + ai * ts).astype(BF16)
    o_ref[0, 1] = (ai * tc - ar * ts).astype(BF16)


def _fft2_kernel(a_ref, g_ref, c_ref, s_ref, o_ref, *, scale):
    k = a_ref.shape[2]
    n2 = a_ref.shape[3]
    for j in range(k):
        r = jnp.concatenate([a_ref[0, 0, j], a_ref[0, 1, j]], axis=0)
        z = jnp.dot(g_ref[...], r, preferred_element_type=F32)
        y = (jnp.dot(z[:n2].astype(BF16), c_ref[...], preferred_element_type=F32)
             + jnp.dot(z[n2:].astype(BF16), s_ref[...], preferred_element_type=F32))
        o_ref[0, :, j * MIXER_W:(j + 1) * MIXER_W] = (y * scale).astype(BF16)


def _dft_cs(n):
    j = np.arange(n)
    ang = 2.0 * np.pi * ((j[:, None] * j[None, :]) % n) / n
    return np.cos(ang), np.sin(ang)


def _fourier(fu, col_block, slabs):
    b, s, w = fu.shape
    n2 = FFT_INNER
    n1 = s // n2
    c1, s1 = _dft_cs(n1)
    f1 = jnp.asarray(np.concatenate([c1, -s1], axis=0), F32).astype(BF16)
    ang = 2.0 * np.pi * (np.arange(n1)[:, None] * np.arange(n2)[None, :]) / s
    tw_c = jnp.repeat(jnp.asarray(np.cos(ang), F32), w, axis=1)
    tw_s = jnp.repeat(jnp.asarray(np.sin(ang), F32), w, axis=1)
    c2, s2 = _dft_cs(n2)
    g2 = jnp.asarray(np.block([[c2, s2], [-s2, c2]]), F32).astype(BF16)
    cc, sc = _dft_cs(HEAD_DIM)
    eye = np.eye(w // HEAD_DIM)
    cbd = jnp.asarray(np.kron(eye, cc), F32).astype(BF16)
    sbd = jnp.asarray(np.kron(eye, sc), F32).astype(BF16)

    cols = n2 * w
    x = fu.reshape(b, n1, cols)
    a = pl.pallas_call(
        _fft1_kernel,
        grid=(cols // col_block, b),
        in_specs=[pl.BlockSpec((1, n1, col_block), lambda j, i: (i, 0, j)),
                  _const_spec((2 * n1, n1)),
                  pl.BlockSpec((n1, col_block), lambda j, i: (0, j)),
                  pl.BlockSpec((n1, col_block), lambda j, i: (0, j))],
        out_specs=pl.BlockSpec((1, 2, n1, col_block), lambda j, i: (i, 0, 0, j)),
        out_shape=jax.ShapeDtypeStruct((b, 2, n1, cols), BF16),
        compiler_params=_params("parallel", "parallel"),
    )(x, f1, tw_c, tw_s)
    a = a.reshape(b, 2, n1, n2, w)
    y = pl.pallas_call(
        functools.partial(_fft2_kernel, scale=float((s * HEAD_DIM) ** -0.5)),
        grid=(b, n1 // slabs),
        in_specs=[pl.BlockSpec((1, 2, slabs, n2, w), lambda i, j: (i, 0, j, 0, 0)),
                  _const_spec((2 * n2, 2 * n2)), _const_spec((w, w)), _const_spec((w, w))],
        out_specs=pl.BlockSpec((1, n2, slabs * w), lambda i, j: (i, 0, j)),
        out_shape=jax.ShapeDtypeStruct((b, n2, n1 * w), BF16),
        compiler_params=_params("parallel", "parallel"),
    )(a, g2, cbd, sbd)
    return y.reshape(b, s, w)


def _att_kernel(q_ref, kt_ref, v_ref, o_ref, *, kv_block):
    q = q_ref[0]
    tq = q.shape[0]
    s = kt_ref.shape[2]
    head = lax.broadcasted_iota(jnp.int32, q.shape, 1) // HEAD_DIM
    qs = jnp.concatenate([jnp.where(head == h, q, jnp.zeros_like(q)) for h in range(ATT_Q_HEADS)], axis=0)
    m = acc = None
    for j in range(s // kv_block):
        cols = slice(j * kv_block, (j + 1) * kv_block)
        sc = jnp.dot(qs, kt_ref[0, :, cols], preferred_element_type=F32)
        bm = jnp.max(sc, axis=-1, keepdims=True)
        m_new = bm if m is None else jnp.maximum(m, bm)
        pv = jnp.dot(jnp.exp2(sc - m_new).astype(BF16), v_ref[0, cols, :], preferred_element_type=F32)
        acc = pv if m is None else acc * jnp.exp2(m - m_new) + pv
        m = m_new
    o = [acc[h * tq:(h + 1) * tq, :LANES] / acc[h * tq:(h + 1) * tq, LANES:] for h in range(ATT_Q_HEADS)]
    lo = lax.broadcasted_iota(jnp.int32, (tq, LANES), 1) < HEAD_DIM
    out = jnp.concatenate([jnp.where(lo, o[0], pltpu.roll(o[1], HEAD_DIM, 1)),
                           jnp.where(lo, pltpu.roll(o[2], HEAD_DIM, 1), o[3])], axis=-1)
    o_ref[0] = out.astype(BF16)


def _attention(aq, akt, av, tq, kv_block=512):
    b, s, w = aq.shape
    return pl.pallas_call(
        functools.partial(_att_kernel, kv_block=kv_block),
        grid=(b, s // tq),
        in_specs=[pl.BlockSpec((1, tq, w), lambda i, j: (i, j, 0)),
                  pl.BlockSpec((1, w, s), lambda i, j: (i, 0, 0)),
                  pl.BlockSpec((1, s, w), lambda i, j: (i, 0, 0))],
        out_specs=pl.BlockSpec((1, tq, w), lambda i, j: (i, j, 0)),
        out_shape=jax.ShapeDtypeStruct((b, s, w), BF16),
        compiler_params=_params("parallel", "parallel"),
    )(aq, akt, av)


def _post_kernel(x_ref, ret_ref, fft_ref, att_ref, cb_ref, z_ref, halo_ref, cw_ref, wo_ref,
                 nf_ref, w1_ref, w2_ref, np_ref, wg_ref, p_ref, wp_ref, fn_ref, o_ref, *, final, ff_block):
    tm = x_ref.shape[1]
    z = z_ref[0].astype(F32)
    row = lax.broadcasted_iota(jnp.int32, z.shape, 0)
    halo = halo_ref[0, 0].astype(F32)
    z_prev = jnp.where(row == 0, halo[0:1], pltpu.roll(z, 1, 0))
    z_next = jnp.where(row == tm - 1, halo[1:2], pltpu.roll(z, tm - 1, 0))
    cw = cw_ref[...]
    conv = cb_ref[0].astype(F32) * (z_prev * cw[0:1] + z * cw[1:2] + z_next * cw[2:3])
    mixed = jnp.concatenate([ret_ref[0], fft_ref[0], att_ref[0], conv.astype(BF16)], axis=-1)
    x = x_ref[0] + jnp.dot(mixed, wo_ref[...], preferred_element_type=F32)

    hn = _rms(x, nf_ref[...]).astype(BF16)
    acc = x
    for j in range(D_FF // ff_block):
        hf = jnp.dot(hn, w1_ref[:, j * ff_block:(j + 1) * ff_block], preferred_element_type=F32)
        act = jnp.square(jnp.maximum(hf, 0.0)).astype(BF16)
        acc = acc + jnp.dot(act, w2_ref[j * ff_block:(j + 1) * ff_block, :], preferred_element_type=F32)
    x = acc

    gate = jax.nn.sigmoid(jnp.dot(_rms(x, np_ref[...]).astype(BF16), wg_ref[...], preferred_element_type=F32))
    x = x + gate * jnp.dot(p_ref[0].astype(BF16), wp_ref[...], preferred_element_type=F32)
    if final:
        x = _rms(x, fn_ref[...])
    o_ref[0] = x


def _post(x, y_ret, y_fft, y_att, cb, z, conv_w, w_out, nf, w1, w2, npl, wg, p, wp, fn, tm, final):
    b, s, _ = x.shape
    nt = s // tm
    zero = jnp.zeros((b, 1, MIXER_W), z.dtype)
    prev = jnp.concatenate([zero, z[:, tm - 1::tm][:, :-1]], axis=1)
    nxt = jnp.concatenate([z[:, ::tm][:, 1:], zero], axis=1)
    halo = jnp.concatenate([prev[:, :, None], nxt[:, :, None],
                            jnp.zeros((b, nt, 6, MIXER_W), z.dtype)], axis=2)
    tok = pl.BlockSpec((1, tm, MIXER_W), lambda i, j: (i, j, 0))
    xs = pl.BlockSpec((1, tm, D_MODEL), lambda i, j: (i, j, 0))
    return pl.pallas_call(
        functools.partial(_post_kernel, final=final, ff_block=1024),
        grid=(b, nt),
        in_specs=[xs, tok, tok, tok, tok, tok,
                  pl.BlockSpec((1, 1, 8, MIXER_W), lambda i, j: (i, j, 0, 0)),
                  _const_spec((3, MIXER_W)), _const_spec((D_MODEL, D_MODEL)),
                  _const_spec((1, D_MODEL)), _const_spec((D_MODEL, D_FF)), _const_spec((D_FF, D_MODEL)),
                  _const_spec((1, D_MODEL)), _const_spec((D_MODEL, D_MODEL)),
                  pl.BlockSpec((1, tm, PLE_DIM), lambda i, j: (i, j, 0)),
                  _const_spec((PLE_DIM, D_MODEL)), _const_spec((1, D_MODEL))],
        out_specs=xs,
        out_shape=jax.ShapeDtypeStruct((b, s, D_MODEL), F32),
        compiler_params=_params("parallel", "parallel"),
    )(x, y_ret, y_fft, y_att, cb, z, halo, conv_w, w_out, nf, w1, w2, npl, wg, p, wp, fn)


def _rope_tables(s):
    t = jnp.arange(s, dtype=F32)
    half = HEAD_DIM // 2
    inv = ROPE_BASE ** (-jnp.arange(half, dtype=F32) / half)
    ang = t[:, None] * inv[None, :]
    cos, sin = jnp.cos(ang), jnp.sin(ang)
    rcos = jnp.tile(jnp.concatenate([cos, cos], axis=-1), (1, RET_HEADS))
    rsin = jnp.tile(jnp.concatenate([-sin, sin], axis=-1), (1, RET_HEADS))
    quarter = half // 2
    inv_a = ROPE_BASE ** (-jnp.arange(quarter, dtype=F32) / quarter)
    ang_r = (jnp.arange(s) // GRID_W).astype(F32)[:, None] * inv_a[None, :]
    ang_c = (jnp.arange(s) % GRID_W).astype(F32)[:, None] * inv_a[None, :]
    cr, sr, cc, sc = jnp.cos(ang_r), jnp.sin(ang_r), jnp.cos(ang_c), jnp.sin(ang_c)
    acos = jnp.tile(jnp.concatenate([cr, cr, cc, cc], axis=-1), (1, ATT_Q_HEADS))
    asin = jnp.tile(jnp.concatenate([-sr, sr, -sc, sc], axis=-1), (1, ATT_Q_HEADS))
    return rcos, rsin, acos, asin


def _encode(x, p, wts, depth):
    b, s, _ = x.shape
    tabs = _rope_tables(s)
    hd = np.arange(MIXER_W) // HEAD_DIM
    gmean = jnp.asarray((hd[:, None] == hd[None, :]) / HEAD_DIM, F32).astype(BF16)
    for l in range(depth):
        w = wts[l]
        rq, rkt, rv, rg, fu, aq, akt, av, cb, z = _in_proj(
            x, w["norm_mix"], w["w_in"], tabs, w["q_norm"], w["k_norm"], gmean, tm=512)
        y_ret = _retention(rq, rkt, rv, rg, w["ret_tabs"], gmean, w["ret_gn"])
        y_fft = _fourier(fu, col_block=4096, slabs=8)
        y_att = _attention(aq, akt, av, tq=128)
        x = _post(x, y_ret, y_fft, y_att, cb, z, w["conv_w"], w["w_out"], w["norm_ffn"], w["w_ffn1"],
                  w["w_ffn2"], w["norm_pl"], w["w_pl_gate"], p[l], w["w_pl_proj"], w["final_norm"],
                  tm=512, final=(l == depth - 1))
    return x


def kernel(x_prompt, x_sample, p_prompt, p_sample, norm_mix_w, w_in, ret_log_rate, ret_gn_w, q_norm_w, k_norm_w,
           conv_w, w_out, norm_ffn_w, w_ffn1, w_ffn2, norm_pl_w, w_pl_gate, w_pl_proj, final_norm_w):
    depth = w_in.shape[0]
    wts = []
    for l in range(depth):
        wts.append(dict(
            norm_mix=norm_mix_w[l][None, :], w_in=w_in[l].astype(BF16),
            ret_tabs=_retention_tables(ret_log_rate[l]), ret_gn=ret_gn_w[l][None, :],
            q_norm=jnp.tile(q_norm_w[l], ATT_Q_HEADS)[None, :], k_norm=jnp.tile(k_norm_w[l], ATT_KV_HEADS)[None, :],
            conv_w=conv_w[l], w_out=w_out[l].astype(BF16), norm_ffn=norm_ffn_w[l][None, :],
            w_ffn1=w_ffn1[l].astype(BF16), w_ffn2=w_ffn2[l].astype(BF16), norm_pl=norm_pl_w[l][None, :],
            w_pl_gate=w_pl_gate[l].astype(BF16), w_pl_proj=w_pl_proj[l].astype(BF16),
            final_norm=final_norm_w[None, :]))
    return _encode(x_prompt, p_prompt, wts, depth), _encode(x_sample, p_sample, wts, depth)
```

```python
import functools

import numpy as np
import jax
import jax.numpy as jnp
from jax import lax
from jax.experimental import pallas as pl
from jax.experimental.pallas import tpu as pltpu

D_MODEL = 1024
HEAD_DIM = 64
RET_HEADS = 4
ATT_Q_HEADS = 4
ATT_KV_HEADS = 2
MIXER_W = 256
ATT_KV_W = ATT_KV_HEADS * HEAD_DIM
PROJ_W = 2560
D_FF = 4 * D_MODEL
PLE_DIM = 256
GRID_W = 64
RET_CHUNK = 128
ROPE_BASE = 10000.0
NORM_EPS = 1e-6
LOG2_E = 1.4426950408889634
FFT_INNER = 64

LANES = 128
VMEM_LIMIT = 56 * 1024 * 1024

F32 = jnp.float32
BF16 = jnp.bfloat16

_O_RQ, _O_RK, _O_RV, _O_RG, _O_FU, _O_AQ, _O_AK, _O_AV, _O_CB, _O_CC, _O_CH = (
    0, 256, 512, 768, 1024, 1280, 1536, 1664, 1792, 2048, 2304)


def _params(*sem):
    return pltpu.CompilerParams(dimension_semantics=sem, vmem_limit_bytes=VMEM_LIMIT)


def _const_spec(shape):
    nd = len(shape)
    return pl.BlockSpec(shape, lambda *_: (0,) * nd, pipeline_mode=pl.Buffered(1))


def _rms(x, w):
    ms = jnp.mean(x * x, axis=-1, keepdims=True)
    return x * lax.rsqrt(ms + NORM_EPS) * w


def _pair_swap(v, half):
    parts = []
    for s in range(v.shape[-1] // LANES):
        vs = v[:, s * LANES:(s + 1) * LANES]
        lane = lax.broadcasted_iota(jnp.int32, vs.shape, 1)
        lo = (lane & half) == 0
        parts.append(jnp.where(lo, pltpu.roll(vs, LANES - half, 1), pltpu.roll(vs, half, 1)))
    return parts[0] if len(parts) == 1 else jnp.concatenate(parts, axis=-1)


def _dup_heads(v):
    sw = pltpu.roll(v, HEAD_DIM, 1)
    lane = lax.broadcasted_iota(jnp.int32, v.shape, 1)
    lo = lane < HEAD_DIM
    return jnp.concatenate([jnp.where(lo, v, sw), jnp.where(lo, sw, v)], axis=-1)


def _in_kernel(x_ref, nw_ref, w_ref, rcos_ref, rsin_ref, acos_ref, asin_ref, qw_ref, kw_ref, g_ref,
               rq_ref, rkt_ref, rv_ref, rg_ref, fu_ref, aq_ref, akt_ref, av_ref, cb_ref, z_ref):
    h = _rms(x_ref[0], nw_ref[...]).astype(BF16)

    def seg(off, width=MIXER_W):
        return jnp.dot(h, w_ref[:, off:off + width], preferred_element_type=F32)

    rcos, rsin = rcos_ref[...], rsin_ref[...]
    q = seg(_O_RQ)
    rq_ref[0] = (q * rcos + _pair_swap(q, HEAD_DIM // 2) * rsin).astype(BF16)
    k = seg(_O_RK)
    k = (k * rcos + _pair_swap(k, HEAD_DIM // 2) * rsin) * (HEAD_DIM ** -0.5)
    rkt_ref[0] = k.T.astype(BF16)
    rv_ref[0] = seg(_O_RV).astype(BF16)
    g = seg(_O_RG)
    rg_ref[0] = (g * jax.nn.sigmoid(g)).astype(BF16)
    fu_ref[0] = seg(_O_FU).astype(BF16)

    acos, asin = acos_ref[...], asin_ref[...]
    gm = g_ref[...]
    q = seg(_O_AQ)
    ms = jnp.dot((q * q).astype(BF16), gm, preferred_element_type=F32)
    q = q * lax.rsqrt(ms + NORM_EPS) * qw_ref[...]
    q = (q * acos + _pair_swap(q, HEAD_DIM // 4) * asin) * (HEAD_DIM ** -0.5 * LOG2_E)
    aq_ref[0] = q.astype(BF16)
    k = seg(_O_AK, ATT_KV_W)
    ms = jnp.dot((k * k).astype(BF16), gm[:ATT_KV_W, :ATT_KV_W], preferred_element_type=F32)
    k = k * lax.rsqrt(ms + NORM_EPS) * kw_ref[...]
    k = k * acos[:, :ATT_KV_W] + _pair_swap(k, HEAD_DIM // 4) * asin[:, :ATT_KV_W]
    akt_ref[0] = _dup_heads(k).T.astype(BF16)
    v = seg(_O_AV, ATT_KV_W)
    av_ref[0] = jnp.concatenate([v, jnp.ones_like(v)], axis=-1).astype(BF16)

    cb_ref[0] = seg(_O_CB).astype(BF16)
    z_ref[0] = (seg(_O_CC) * seg(_O_CH)).astype(BF16)


def _in_proj(x, nw, w_in, tabs, qw, kw, gmean, tm):
    b, s, _ = x.shape
    rcos, rsin, acos, asin = tabs
    tok = pl.BlockSpec((1, tm, MIXER_W), lambda i, j: (i, j, 0))
    tok_t = pl.BlockSpec((1, MIXER_W, tm), lambda i, j: (i, 0, j))
    tab = pl.BlockSpec((tm, MIXER_W), lambda i, j: (j, 0))
    nat = jax.ShapeDtypeStruct((b, s, MIXER_W), BF16)
    tr = jax.ShapeDtypeStruct((b, MIXER_W, s), BF16)
    return pl.pallas_call(
        _in_kernel,
        grid=(b, s // tm),
        in_specs=[pl.BlockSpec((1, tm, D_MODEL), lambda i, j: (i, j, 0)),
                  _const_spec((1, D_MODEL)), _const_spec((D_MODEL, PROJ_W)),
                  tab, tab, tab, tab,
                  _const_spec((1, MIXER_W)), _const_spec((1, ATT_KV_W)), _const_spec((MIXER_W, MIXER_W))],
        out_specs=[tok, tok_t, tok, tok, tok, tok, tok_t, tok, tok, tok],
        out_shape=[nat, tr, nat, nat, nat, nat, tr, nat, nat, nat],
        compiler_params=_params("parallel", "parallel"),
    )(x, nw, w_in, rcos, rsin, acos, asin, qw, kw, gmean)


def _ret_kernel(q_ref, kt_ref, v_ref, g_ref, dec_ref, zf_ref, zb_ref, xf_ref, xb_ref, gf_ref, gb_ref,
                bd_ref, gm_ref, gnw_ref, o_ref, sf_ref, sb_ref, curf_ref, curb_ref, p_ref, acc_ref, dev_ref):
    c = RET_CHUNK
    s = q_ref.shape[1]
    nc = s // c
    head = lax.broadcasted_iota(jnp.int32, (c, MIXER_W), 1) // HEAD_DIM

    def rows(n, size=c):
        return pl.ds(pl.multiple_of(n * size, size), size)

    curf_ref[...] = jnp.zeros_like(curf_ref)
    curb_ref[...] = jnp.zeros_like(curb_ref)

    def scan(i, carry):
        for n, cur, hist, zeta, gdec in ((i, curf_ref, sf_ref, zf_ref, gf_ref),
                                         (nc - 1 - i, curb_ref, sb_ref, zb_ref, gb_ref)):
            r = rows(n)
            state = cur[...]
            hist[n] = state.astype(BF16)
            kz = (kt_ref[0, :, r].astype(F32) * zeta[...]).astype(BF16)
            cur[...] = state * gdec[...] + jnp.dot(kz, v_ref[0, r, :], preferred_element_type=F32) * bd_ref[...]
        r = rows(i)
        qc = q_ref[0, r, :]
        qs = jnp.concatenate([jnp.where(head == h, qc, jnp.zeros_like(qc)) for h in range(RET_HEADS)], axis=0)
        p_ref[i] = (jnp.dot(qs, kt_ref[0, :, r], preferred_element_type=F32) * dec_ref[...]).astype(BF16)
        return carry

    lax.fori_loop(0, nc, scan, 0, unroll=4)

    def mix(n, carry):
        r = rows(n)
        qc = q_ref[0, r, :]
        inner = jnp.dot(p_ref[n], v_ref[0, r, :], preferred_element_type=F32)
        o = (jnp.dot(qc, sf_ref[n], preferred_element_type=F32) * xf_ref[...]
             + jnp.dot(qc, sb_ref[n], preferred_element_type=F32) * xb_ref[...])
        for h in range(RET_HEADS):
            o = o + jnp.where(head == h, inner[h * c:(h + 1) * c], 0.0)
        acc_ref[r, :] = o
        return carry

    lax.fori_loop(0, nc, mix, 0, unroll=4)

    nb = 4 * c

    def center(n, carry):
        r = rows(n, nb)
        o = acc_ref[r, :]
        hi = o.astype(BF16)
        lo = (o - hi.astype(F32)).astype(BF16)
        gm = gm_ref[...]
        dev_ref[r, :] = o - (jnp.dot(hi, gm, preferred_element_type=F32) + jnp.dot(lo, gm, preferred_element_type=F32))
        return carry

    lax.fori_loop(0, s // nb, center, 0, unroll=2)

    def scale(n, carry):
        r = rows(n, nb)
        d = dev_ref[r, :]
        var = jnp.dot((d * d).astype(BF16), gm_ref[...], preferred_element_type=F32)
        y = d * lax.rsqrt(var + NORM_EPS) * gnw_ref[...]
        o_ref[0, r, :] = (g_ref[0, r, :].astype(F32) * y).astype(BF16)
        return carry

    lax.fori_loop(0, s // nb, scale, 0, unroll=2)


def _retention(rq, rkt, rv, rg, tabs, gmean, gnw):
    b, s, _ = rq.shape
    nat = pl.BlockSpec((1, s, MIXER_W), lambda i: (i, 0, 0))
    tr = pl.BlockSpec((1, MIXER_W, s), lambda i: (i, 0, 0))
    return pl.pallas_call(
        _ret_kernel,
        grid=(b,),
        in_specs=[nat, tr, nat, nat] + [_const_spec(t.shape) for t in tabs]
                 + [_const_spec((MIXER_W, MIXER_W)), _const_spec((1, MIXER_W))],
        out_specs=nat,
        out_shape=jax.ShapeDtypeStruct((b, s, MIXER_W), BF16),
        scratch_shapes=[pltpu.VMEM((s // RET_CHUNK, MIXER_W, MIXER_W), BF16),
                        pltpu.VMEM((s // RET_CHUNK, MIXER_W, MIXER_W), BF16),
                        pltpu.VMEM((MIXER_W, MIXER_W), F32), pltpu.VMEM((MIXER_W, MIXER_W), F32),
                        pltpu.VMEM((s // RET_CHUNK, RET_HEADS * RET_CHUNK, RET_CHUNK), BF16),
                        pltpu.VMEM((s, MIXER_W), F32), pltpu.VMEM((s, MIXER_W), F32)],
        compiler_params=_params("parallel"),
    )(rq, rkt, rv, rg, *tabs, gmean, gnw)


def _retention_tables(log_rate):
    c = RET_CHUNK
    log_g = -jnp.exp(log_rate.astype(F32))
    lf, lb = log_g[0], log_g[1]
    idx = jnp.arange(c, dtype=F32)
    diff = idx[:, None] - idx[None, :]
    low = diff >= 0
    up = diff < 0
    dec_f = jnp.where(low[None], jnp.exp(jnp.where(low, diff, 0.0)[None] * lf[:, None, None]), 0.0)
    dec_b = jnp.where(up[None], jnp.exp(jnp.where(up, -diff, 0.0)[None] * lb[:, None, None]), 0.0)
    dec = (dec_f + dec_b).reshape(RET_HEADS * c, c)
    lane = lambda per_head: jnp.repeat(per_head, HEAD_DIM, axis=-1)
    zeta_f = lane(jnp.exp((c - 1 - idx)[:, None] * lf[None, :])).T
    zeta_b = lane(jnp.exp(idx[:, None] * lb[None, :])).T
    xi_f = lane(jnp.exp((idx + 1.0)[:, None] * lf[None, :]))
    xi_b = lane(jnp.exp((c - idx)[:, None] * lb[None, :]))
    hd = jnp.arange(MIXER_W) // HEAD_DIM
    bd = (hd[:, None] == hd[None, :]).astype(F32)
    g_f = bd * lane(jnp.exp(c * lf)[None, :])
    g_b = bd * lane(jnp.exp(c * lb)[None, :])
    return dec, zeta_f, zeta_b, xi_f, xi_b, g_f, g_b, bd


def _fft1_kernel(x_ref, f_ref, tc_ref, ts_ref, o_ref):
    n1 = x_ref.shape[1]
    a = jnp.dot(f_ref[...], x_ref[0], preferred_element_type=F32)
    ar, ai = a[:n1], a[n1:]
    tc, ts = tc_ref[...], ts_ref[...]
    o_ref[0, 0] = (ar * tc + ai * ts).astype(BF16)
    o_ref[0, 1] = (ai * tc - ar * ts).astype(BF16)


def _fft2_kernel(a_ref, g_ref, c_ref, s_ref, o_ref, *, scale):
    k = a_ref.shape[2]
    n2 = a_ref.shape[3]
    for j in range(k):
        r = jnp.concatenate([a_ref[0, 0, j], a_ref[0, 1, j]], axis=0)
        z = jnp.dot(g_ref[...], r, preferred_element_type=F32)
        y = (jnp.dot(z[:n2].astype(BF16), c_ref[...], preferred_element_type=F32)
             + jnp.dot(z[n2:].astype(BF16), s_ref[...], preferred_element_type=F32))
        o_ref[0, :, j * MIXER_W:(j + 1) * MIXER_W] = (y * scale).astype(BF16)


def _dft_cs(n):
    j = np.arange(n)
    ang = 2.0 * np.pi * ((j[:, None] * j[None, :]) % n) / n
    return np.cos(ang), np.sin(ang)


def _fourier(fu, col_block, slabs):
    b, s, w = fu.shape
    n2 = FFT_INNER
    n1 = s // n2
    c1, s1 = _dft_cs(n1)
    f1 = jnp.asarray(np.concatenate([c1, -s1], axis=0), F32).astype(BF16)
    ang = 2.0 * np.pi * (np.arange(n1)[:, None] * np.arange(n2)[None, :]) / s
    tw_c = jnp.repeat(jnp.asarray(np.cos(ang), F32), w, axis=1)
    tw_s = jnp.repeat(jnp.asarray(np.sin(ang), F32), w, axis=1)
    c2, s2 = _dft_cs(n2)
    g2 = jnp.asarray(np.block([[c2, s2], [-s2, c2]]), F32).astype(BF16)
    cc, sc = _dft_cs(HEAD_DIM)
    eye = np.eye(w // HEAD_DIM)
    cbd = jnp.asarray(np.kron(eye, cc), F32).astype(BF16)
    sbd = jnp.asarray(np.kron(eye, sc), F32).astype(BF16)

    cols = n2 * w
    x = fu.reshape(b, n1, cols)
    a = pl.pallas_call(
        _fft1_kernel,
        grid=(cols // col_block, b),
        in_specs=[pl.BlockSpec((1, n1, col_block), lambda j, i: (i, 0, j)),
                  _const_spec((2 * n1, n1)),
                  pl.BlockSpec((n1, col_block), lambda j, i: (0, j)),
                  pl.BlockSpec((n1, col_block), lambda j, i: (0, j))],
        out_specs=pl.BlockSpec((1, 2, n1, col_block), lambda j, i: (i, 0, 0, j)),
        out_shape=jax.ShapeDtypeStruct((b, 2, n1, cols), BF16),
        compiler_params=_params("parallel", "parallel"),
    )(x, f1, tw_c, tw_s)
    a = a.reshape(b, 2, n1, n2, w)
    y = pl.pallas_call(
        functools.partial(_fft2_kernel, scale=float((s * HEAD_DIM) ** -0.5)),
        grid=(b, n1 // slabs),
        in_specs=[pl.BlockSpec((1, 2, slabs, n2, w), lambda i, j: (i, 0, j, 0, 0)),
                  _const_spec((2 * n2, 2 * n2)), _const_spec((w, w)), _const_spec((w, w))],
        out_specs=pl.BlockSpec((1, n2, slabs * w), lambda i, j: (i, 0, j)),
        out_shape=jax.ShapeDtypeStruct((b, n2, n1 * w), BF16),
        compiler_params=_params("parallel", "parallel"),
    )(a, g2, cbd, sbd)
    return y.reshape(b, s, w)


def _att_kernel(q_ref, kt_ref, v_ref, o_ref, *, kv_block):
    q = q_ref[0]
    tq = q.shape[0]
    s = kt_ref.shape[2]
    head = lax.broadcasted_iota(jnp.int32, q.shape, 1) // HEAD_DIM
    qs = jnp.concatenate([jnp.where(head == h, q, jnp.zeros_like(q)) for h in range(ATT_Q_HEADS)], axis=0)
    m = acc = None
    for j in range(s // kv_block):
        cols = slice(j * kv_block, (j + 1) * kv_block)
        sc = jnp.dot(qs, kt_ref[0, :, cols], preferred_element_type=F32)
        bm = jnp.max(sc, axis=-1, keepdims=True)
        m_new = bm if m is None else jnp.maximum(m, bm)
        pv = jnp.dot(jnp.exp2(sc - m_new).astype(BF16), v_ref[0, cols, :], preferred_element_type=F32)
        acc = pv if m is None else acc * jnp.exp2(m - m_new) + pv
        m = m_new
    o = [acc[h * tq:(h + 1) * tq, :LANES] / acc[h * tq:(h + 1) * tq, LANES:] for h in range(ATT_Q_HEADS)]
    lo = lax.broadcasted_iota(jnp.int32, (tq, LANES), 1) < HEAD_DIM
    out = jnp.concatenate([jnp.where(lo, o[0], pltpu.roll(o[1], HEAD_DIM, 1)),
                           jnp.where(lo, pltpu.roll(o[2], HEAD_DIM, 1), o[3])], axis=-1)
    o_ref[0] = out.astype(BF16)


def _attention(aq, akt, av, tq, kv_block=512):
    b, s, w = aq.shape
    return pl.pallas_call(
        functools.partial(_att_kernel, kv_block=kv_block),
        grid=(b, s // tq),
        in_specs=[pl.BlockSpec((1, tq, w), lambda i, j: (i, j, 0)),
                  pl.BlockSpec((1, w, s), lambda i, j: (i, 0, 0)),
                  pl.BlockSpec((1, s, w), lambda i, j: (i, 0, 0))],
        out_specs=pl.BlockSpec((1, tq, w), lambda i, j: (i, j, 0)),
        out_shape=jax.ShapeDtypeStruct((b, s, w), BF16),
        compiler_params=_params("parallel", "parallel"),
    )(aq, akt, av)


def _post_kernel(x_ref, ret_ref, fft_ref, att_ref, cb_ref, z_ref, halo_ref, cw_ref, wo_ref,
                 nf_ref, w1_ref, w2_ref, np_ref, wg_ref, p_ref, wp_ref, fn_ref, o_ref, *, final, ff_block):
    tm = x_ref.shape[1]
    z = z_ref[0].astype(F32)
    row = lax.broadcasted_iota(jnp.int32, z.shape, 0)
    halo = halo_ref[0, 0].astype(F32)
    z_prev = jnp.where(row == 0, halo[0:1], pltpu.roll(z, 1, 0))
    z_next = jnp.where(row == tm - 1, halo[1:2], pltpu.roll(z, tm - 1, 0))
    cw = cw_ref[...]
    conv = cb_ref[0].astype(F32) * (z_prev * cw[0:1] + z * cw[1:2] + z_next * cw[2:3])
    mixed = jnp.concatenate([ret_ref[0], fft_ref[0], att_ref[0], conv.astype(BF16)], axis=-1)
    x = x_ref[0] + jnp.dot(mixed, wo_ref[...], preferred_element_type=F32)

    hn = _rms(x, nf_ref[...]).astype(BF16)
    acc = x
    for j in range(D_FF // ff_block):
        hf = jnp.dot(hn, w1_ref[:, j * ff_block:(j + 1) * ff_block], preferred_element_type=F32)
        act = jnp.square(jnp.maximum(hf, 0.0)).astype(BF16)
        acc = acc + jnp.dot(act, w2_ref[j * ff_block:(j + 1) * ff_block, :], preferred_element_type=F32)
    x = acc

    gate = jax.nn.sigmoid(jnp.dot(_rms(x, np_ref[...]).astype(BF16), wg_ref[...], preferred_element_type=F32))
    x = x + gate * jnp.dot(p_ref[0].astype(BF16), wp_ref[...], preferred_element_type=F32)
    if final:
        x = _rms(x, fn_ref[...])
    o_ref[0] = x


def _post(x, y_ret, y_fft, y_att, cb, z, conv_w, w_out, nf, w1, w2, npl, wg, p, wp, fn, tm, final):
    b, s, _ = x.shape
    nt = s // tm
    zero = jnp.zeros((b, 1, MIXER_W), z.dtype)
    prev = jnp.concatenate([zero, z[:, tm - 1::tm][:, :-1]], axis=1)
    nxt = jnp.concatenate([z[:, ::tm][:, 1:], zero], axis=1)
    halo = jnp.concatenate([prev[:, :, None], nxt[:, :, None],
                            jnp.zeros((b, nt, 6, MIXER_W), z.dtype)], axis=2)
    tok = pl.BlockSpec((1, tm, MIXER_W), lambda i, j: (i, j, 0))
    xs = pl.BlockSpec((1, tm, D_MODEL), lambda i, j: (i, j, 0))
    return pl.pallas_call(
        functools.partial(_post_kernel, final=final, ff_block=1024),
        grid=(b, nt),
        in_specs=[xs, tok, tok, tok, tok, tok,
                  pl.BlockSpec((1, 1, 8, MIXER_W), lambda i, j: (i, j, 0, 0)),
                  _const_spec((3, MIXER_W)), _const_spec((D_MODEL, D_MODEL)),
                  _const_spec((1, D_MODEL)), _const_spec((D_MODEL, D_FF)), _const_spec((D_FF, D_MODEL)),
                  _const_spec((1, D_MODEL)), _const_spec((D_MODEL, D_MODEL)),
                  pl.BlockSpec((1, tm, PLE_DIM), lambda i, j: (i, j, 0)),
                  _const_spec((PLE_DIM, D_MODEL)), _const_spec((1, D_MODEL))],
        out_specs=xs,
        out_shape=jax.ShapeDtypeStruct((b, s, D_MODEL), F32),
        compiler_params=_params("parallel", "parallel"),
    )(x, y_ret, y_fft, y_att, cb, z, halo, conv_w, w_out, nf, w1, w2, npl, wg, p, wp, fn)


def _rope_tables(s):
    t = jnp.arange(s, dtype=F32)
    half = HEAD_DIM // 2
    inv = ROPE_BASE ** (-jnp.arange(half, dtype=F32) / half)
    ang = t[:, None] * inv[None, :]
    cos, sin = jnp.cos(ang), jnp.sin(ang)
    rcos = jnp.tile(jnp.concatenate([cos, cos], axis=-1), (1, RET_HEADS))
    rsin = jnp.tile(jnp.concatenate([-sin, sin], axis=-1), (1, RET_HEADS))
    quarter = half // 2
    inv_a = ROPE_BASE ** (-jnp.arange(quarter, dtype=F32) / quarter)
    ang_r = (jnp.arange(s) // GRID_W).astype(F32)[:, None] * inv_a[None, :]
    ang_c = (jnp.arange(s) % GRID_W).astype(F32)[:, None] * inv_a[None, :]
    cr, sr, cc, sc = jnp.cos(ang_r), jnp.sin(ang_r), jnp.cos(ang_c), jnp.sin(ang_c)
    acos = jnp.tile(jnp.concatenate([cr, cr, cc, cc], axis=-1), (1, ATT_Q_HEADS))
    asin = jnp.tile(jnp.concatenate([-sr, sr, -sc, sc], axis=-1), (1, ATT_Q_HEADS))
    return rcos, rsin, acos, asin


def _encode(x, p, wts, depth):
    b, s, _ = x.shape
    tabs = _rope_tables(s)
    hd = np.arange(MIXER_W) // HEAD_DIM
    gmean = jnp.asarray((hd[:, None] == hd[None, :]) / HEAD_DIM, F32).astype(BF16)
    for l in range(depth):
        w = wts[l]
        rq, rkt, rv, rg, fu, aq, akt, av, cb, z = _in_proj(
            x, w["norm_mix"], w["w_in"], tabs, w["q_norm"], w["k_norm"], gmean, tm=1024)
        y_ret = _retention(rq, rkt, rv, rg, w["ret_tabs"], gmean, w["ret_gn"])
        y_fft = _fourier(fu, col_block=4096, slabs=16)
        y_att = _attention(aq, akt, av, tq=512, kv_block=256)
        x = _post(x, y_ret, y_fft, y_att, cb, z, w["conv_w"], w["w_out"], w["norm_ffn"], w["w_ffn1"],
                  w["w_ffn2"], w["norm_pl"], w["w_pl_gate"], p[l], w["w_pl_proj"], w["final_norm"],
                  tm=512, final=(l == depth - 1))
    return x


def kernel(x_prompt, x_sample, p_prompt, p_sample, norm_mix_w, w_in, ret_log_rate, ret_gn_w, q_norm_w, k_norm_w,
           conv_w, w_out, norm_ffn_w, w_ffn1, w_ffn2, norm_pl_w, w_pl_gate, w_pl_proj, final_norm_w):
    depth = w_in.shape[0]
    wts = []
    for l in range(depth):
        wts.append(dict(
            norm_mix=norm_mix_w[l][None, :], w_in=w_in[l].astype(BF16),
            ret_tabs=_retention_tables(ret_log_rate[l]), ret_gn=ret_gn_w[l][None, :],
            q_norm=jnp.tile(q_norm_w[l], ATT_Q_HEADS)[None, :], k_norm=jnp.tile(k_norm_w[l], ATT_KV_HEADS)[None, :],
            conv_w=conv_w[l], w_out=w_out[l].astype(BF16), norm_ffn=norm_ffn_w[l][None, :],
            w_ffn1=w_ffn1[l].astype(BF16), w_ffn2=w_ffn2[l].astype(BF16), norm_pl=norm_pl_w[l][None, :],
            w_pl_gate=w_pl_gate[l].astype(BF16), w_pl_proj=w_pl_proj[l].astype(BF16),
            final_norm=final_norm_w[None, :]))
    return _encode(x_prompt, p_prompt, wts, depth), _encode(x_sample, p_sample, wts, depth)
```

```python
import functools

import numpy as np
import jax
import jax.numpy as jnp
from jax import lax
from jax.experimental import pallas as pl
from jax.experimental.pallas import tpu as pltpu

D_MODEL = 1024
HEAD_DIM = 64
RET_HEADS = 4
ATT_Q_HEADS = 4
ATT_KV_HEADS = 2
MIXER_W = 256
ATT_KV_W = ATT_KV_HEADS * HEAD_DIM
PROJ_W = 2560
D_FF = 4 * D_MODEL
PLE_DIM = 256
GRID_W = 64
RET_CHUNK = 128
ROPE_BASE = 10000.0
NORM_EPS = 1e-6
LOG2_E = 1.4426950408889634
FFT_INNER = 64
FFT_PAD = 8

LANES = 128
VMEM_LIMIT = 56 * 1024 * 1024

F32 = jnp.float32
BF16 = jnp.bfloat16

_O_RQ, _O_RK, _O_RV, _O_RG, _O_FU, _O_AQ, _O_AK, _O_AV, _O_CB, _O_CC, _O_CH = (
    0, 256, 512, 768, 1024, 1280, 1536, 1664, 1792, 2048, 2304)


def _params(*sem):
    return pltpu.CompilerParams(dimension_semantics=sem, vmem_limit_bytes=VMEM_LIMIT)


def _const_spec(shape):
    nd = len(shape)
    return pl.BlockSpec(shape, lambda *_: (0,) * nd, pipeline_mode=pl.Buffered(1))


def _rms(x, w):
    ms = jnp.mean(x * x, axis=-1, keepdims=True)
    return x * lax.rsqrt(ms + NORM_EPS) * w


def _pair_swap(v, half):
    parts = []
    for s in range(v.shape[-1] // LANES):
        vs = v[:, s * LANES:(s + 1) * LANES]
        lane = lax.broadcasted_iota(jnp.int32, vs.shape, 1)
        lo = (lane & half) == 0
        parts.append(jnp.where(lo, pltpu.roll(vs, LANES - half, 1), pltpu.roll(vs, half, 1)))
    return parts[0] if len(parts) == 1 else jnp.concatenate(parts, axis=-1)


def _dup_heads(v):
    sw = pltpu.roll(v, HEAD_DIM, 1)
    lane = lax.broadcasted_iota(jnp.int32, v.shape, 1)
    lo = lane < HEAD_DIM
    return jnp.concatenate([jnp.where(lo, v, sw), jnp.where(lo, sw, v)], axis=-1)


def _in_kernel(x_ref, nw_ref, w_ref, rcos_ref, rsin_ref, acos_ref, asin_ref, qw_ref, kw_ref, g_ref,
               rq_ref, rkt_ref, rv_ref, rg_ref, fu_ref, aq_ref, akt_ref, av_ref, cb_ref, z_ref):
    h = _rms(x_ref[0], nw_ref[...]).astype(BF16)

    def seg(off, width=MIXER_W):
        return jnp.dot(h, w_ref[:, off:off + width], preferred_element_type=F32)

    rcos, rsin = rcos_ref[...], rsin_ref[...]
    q = seg(_O_RQ)
    rq_ref[0] = (q * rcos + _pair_swap(q, HEAD_DIM // 2) * rsin).astype(BF16)
    k = seg(_O_RK)
    k = (k * rcos + _pair_swap(k, HEAD_DIM // 2) * rsin) * (HEAD_DIM ** -0.5)
    rkt_ref[0] = k.T.astype(BF16)
    rv_ref[0] = seg(_O_RV).astype(BF16)
    g = seg(_O_RG)
    rg_ref[0] = (g * jax.nn.sigmoid(g)).astype(BF16)
    fu_ref[0] = seg(_O_FU)

    acos, asin = acos_ref[...], asin_ref[...]
    gm = g_ref[...]
    q = seg(_O_AQ)
    ms = jnp.dot((q * q).astype(BF16), gm, preferred_element_type=F32)
    q = q * lax.rsqrt(ms + NORM_EPS) * qw_ref[...]
    q = (q * acos + _pair_swap(q, HEAD_DIM // 4) * asin) * (HEAD_DIM ** -0.5 * LOG2_E)
    aq_ref[0] = q.astype(BF16)
    k = seg(_O_AK, ATT_KV_W)
    ms = jnp.dot((k * k).astype(BF16), gm[:ATT_KV_W, :ATT_KV_W], preferred_element_type=F32)
    k = k * lax.rsqrt(ms + NORM_EPS) * kw_ref[...]
    k = k * acos[:, :ATT_KV_W] + _pair_swap(k, HEAD_DIM // 4) * asin[:, :ATT_KV_W]
    akt_ref[0] = _dup_heads(k).T.astype(BF16)
    v = seg(_O_AV, ATT_KV_W)
    av_ref[0] = jnp.concatenate([v, jnp.ones_like(v)], axis=-1).astype(BF16)

    cb_ref[0] = seg(_O_CB).astype(BF16)
    z_ref[0] = (seg(_O_CC) * seg(_O_CH)).astype(BF16)


def _in_proj(x, nw, w_in, tabs, qw, kw, gmean, tm):
    b, s, _ = x.shape
    rcos, rsin, acos, asin = tabs
    tok = pl.BlockSpec((1, tm, MIXER_W), lambda i, j: (i, j, 0))
    tok_t = pl.BlockSpec((1, MIXER_W, tm), lambda i, j: (i, 0, j))
    tab = pl.BlockSpec((tm, MIXER_W), lambda i, j: (j, 0))
    nat = jax.ShapeDtypeStruct((b, s, MIXER_W), BF16)
    tr = jax.ShapeDtypeStruct((b, MIXER_W, s), BF16)
    return pl.pallas_call(
        _in_kernel,
        grid=(b, s // tm),
        in_specs=[pl.BlockSpec((1, tm, D_MODEL), lambda i, j: (i, j, 0)),
                  _const_spec((1, D_MODEL)), _const_spec((D_MODEL, PROJ_W)),
                  tab, tab, tab, tab,
                  _const_spec((1, MIXER_W)), _const_spec((1, ATT_KV_W)), _const_spec((MIXER_W, MIXER_W))],
        out_specs=[tok, tok_t, tok, tok, tok, tok, tok_t, tok, tok, tok],
        out_shape=[nat, tr, nat, nat, jax.ShapeDtypeStruct((b, s, MIXER_W), F32), nat, tr, nat, nat, nat],
        compiler_params=_params("parallel", "parallel"),
    )(x, nw, w_in, rcos, rsin, acos, asin, qw, kw, gmean)


def _ret_kernel(q_ref, kt_ref, v_ref, g_ref, dec_ref, zf_ref, zb_ref, xf_ref, xb_ref, gf_ref, gb_ref,
                bd_ref, gm_ref, gnw_ref, o_ref, sf_ref, sb_ref, curf_ref, curb_ref, p_ref, acc_ref, dev_ref):
    c = RET_CHUNK
    s = q_ref.shape[1]
    nc = s // c
    head = lax.broadcasted_iota(jnp.int32, (c, MIXER_W), 1) // HEAD_DIM

    def rows(n, size=c):
        return pl.ds(pl.multiple_of(n * size, size), size)

    curf_ref[...] = jnp.zeros_like(curf_ref)
    curb_ref[...] = jnp.zeros_like(curb_ref)

    def scan(i, carry):
        for n, cur, hist, zeta, gdec in ((i, curf_ref, sf_ref, zf_ref, gf_ref),
                                         (nc - 1 - i, curb_ref, sb_ref, zb_ref, gb_ref)):
            r = rows(n)
            state = cur[...]
            hist[n] = state.astype(BF16)
            kz = (kt_ref[0, :, r].astype(F32) * zeta[...]).astype(BF16)
            cur[...] = state * gdec[...] + jnp.dot(kz, v_ref[0, r, :], preferred_element_type=F32) * bd_ref[...]
        r = rows(i)
        qc = q_ref[0, r, :]
        qs = jnp.concatenate([jnp.where(head == h, qc, jnp.zeros_like(qc)) for h in range(RET_HEADS)], axis=0)
        p_ref[i] = (jnp.dot(qs, kt_ref[0, :, r], preferred_element_type=F32) * dec_ref[...]).astype(BF16)
        return carry

    lax.fori_loop(0, nc, scan, 0, unroll=4)

    def mix(n, carry):
        r = rows(n)
        qc = q_ref[0, r, :]
        inner = jnp.dot(p_ref[n], v_ref[0, r, :], preferred_element_type=F32)
        o = (jnp.dot(qc, sf_ref[n], preferred_element_type=F32) * xf_ref[...]
             + jnp.dot(qc, sb_ref[n], preferred_element_type=F32) * xb_ref[...])
        for h in range(RET_HEADS):
            o = o + jnp.where(head == h, inner[h * c:(h + 1) * c], 0.0)
        acc_ref[r, :] = o
        return carry

    lax.fori_loop(0, nc, mix, 0, unroll=4)

    nb = 4 * c

    def center(n, carry):
        r = rows(n, nb)
        o = acc_ref[r, :]
        hi = o.astype(BF16)
        lo = (o - hi.astype(F32)).astype(BF16)
        gm = gm_ref[...]
        dev_ref[r, :] = o - (jnp.dot(hi, gm, preferred_element_type=F32) + jnp.dot(lo, gm, preferred_element_type=F32))
        return carry

    lax.fori_loop(0, s // nb, center, 0, unroll=2)

    def scale(n, carry):
        r = rows(n, nb)
        d = dev_ref[r, :]
        var = jnp.dot((d * d).astype(BF16), gm_ref[...], preferred_element_type=F32)
        y = d * lax.rsqrt(var + NORM_EPS) * gnw_ref[...]
        o_ref[0, r, :] = (g_ref[0, r, :].astype(F32) * y).astype(BF16)
        return carry

    lax.fori_loop(0, s // nb, scale, 0, unroll=2)


def _retention(rq, rkt, rv, rg, tabs, gmean, gnw):
    b, s, _ = rq.shape
    nat = pl.BlockSpec((1, s, MIXER_W), lambda i: (i, 0, 0))
    tr = pl.BlockSpec((1, MIXER_W, s), lambda i: (i, 0, 0))
    return pl.pallas_call(
        _ret_kernel,
        grid=(b,),
        in_specs=[nat, tr, nat, nat] + [_const_spec(t.shape) for t in tabs]
                 + [_const_spec((MIXER_W, MIXER_W)), _const_spec((1, MIXER_W))],
        out_specs=nat,
        out_shape=jax.ShapeDtypeStruct((b, s, MIXER_W), BF16),
        scratch_shapes=[pltpu.VMEM((s // RET_CHUNK, MIXER_W, MIXER_W), BF16),
                        pltpu.VMEM((s // RET_CHUNK, MIXER_W, MIXER_W), BF16),
                        pltpu.VMEM((MIXER_W, MIXER_W), F32), pltpu.VMEM((MIXER_W, MIXER_W), F32),
                        pltpu.VMEM((s // RET_CHUNK, RET_HEADS * RET_CHUNK, RET_CHUNK), BF16),
                        pltpu.VMEM((s, MIXER_W), F32), pltpu.VMEM((s, MIXER_W), F32)],
        compiler_params=_params("parallel"),
    )(rq, rkt, rv, rg, *tabs, gmean, gnw)


def _retention_tables(log_rate):
    c = RET_CHUNK
    log_g = -jnp.exp(log_rate.astype(F32))
    lf, lb = log_g[0], log_g[1]
    idx = jnp.arange(c, dtype=F32)
    diff = idx[:, None] - idx[None, :]
    low = diff >= 0
    up = diff < 0
    dec_f = jnp.where(low[None], jnp.exp(jnp.where(low, diff, 0.0)[None] * lf[:, None, None]), 0.0)
    dec_b = jnp.where(up[None], jnp.exp(jnp.where(up, -diff, 0.0)[None] * lb[:, None, None]), 0.0)
    dec = (dec_f + dec_b).reshape(RET_HEADS * c, c)
    lane = lambda per_head: jnp.repeat(per_head, HEAD_DIM, axis=-1)
    zeta_f = lane(jnp.exp((c - 1 - idx)[:, None] * lf[None, :])).T
    zeta_b = lane(jnp.exp(idx[:, None] * lb[None, :])).T
    xi_f = lane(jnp.exp((idx + 1.0)[:, None] * lf[None, :]))
    xi_b = lane(jnp.exp((c - idx)[:, None] * lb[None, :]))
    hd = jnp.arange(MIXER_W) // HEAD_DIM
    bd = (hd[:, None] == hd[None, :]).astype(F32)
    g_f = bd * lane(jnp.exp(c * lf)[None, :])
    g_b = bd * lane(jnp.exp(c * lb)[None, :])
    return dec, zeta_f, zeta_b, xi_f, xi_b, g_f, g_b, bd


def _fft_kernel(x_ref, f1_ref, tc_ref, ts_ref, g2_ref, c_ref, s_ref, o_ref, xp_ref, are_ref, aim_ref, y_ref, *, scale):
    s = x_ref.shape[1]
    n2 = FFT_INNER
    n1 = s // n2
    p1, p2 = n1 + FFT_PAD, n2 + FFT_PAD
    halves = (slice(0, LANES), slice(LANES, 2 * LANES))

    def put(ref, rows, val):
        for h, lanes in enumerate(halves):
            ref[h, rows, :] = val[:, lanes]

    def get(ref, rows):
        return jnp.concatenate([ref[0, rows, :], ref[1, rows, :]], axis=-1)

    for t1 in range(n1):
        put(xp_ref, slice(t1 * p2, t1 * p2 + n2), x_ref[0, t1 * n2:(t1 + 1) * n2, :])
    f1 = f1_ref[...]
    for t2 in range(n2):
        xs = get(xp_ref, pl.ds(t2, n1, stride=p2)).astype(BF16)
        a = jnp.dot(f1, xs, preferred_element_type=F32)
        ar, ai = a[:n1], a[n1:]
        rows = slice(t2 * n1, (t2 + 1) * n1)
        tc = jnp.concatenate([tc_ref[rows, :]] * 2, axis=-1)
        ts = jnp.concatenate([ts_ref[rows, :]] * 2, axis=-1)
        rows = slice(t2 * p1, t2 * p1 + n1)
        put(are_ref, rows, ar * tc + ai * ts)
        put(aim_ref, rows, ai * tc - ar * ts)
    for f in range(n1):
        rows = pl.ds(f, n2, stride=p1)
        r = jnp.concatenate([get(are_ref, rows), get(aim_ref, rows)], axis=0)
        z = jnp.dot(g2_ref[...], r.astype(BF16), preferred_element_type=F32)
        y = (jnp.dot(z[:n2].astype(BF16), c_ref[...], preferred_element_type=F32)
             + jnp.dot(z[n2:].astype(BF16), s_ref[...], preferred_element_type=F32))
        put(y_ref, slice(f * p2, f * p2 + n2), y * scale)
    for f2 in range(n2):
        o_ref[0, f2 * n1:(f2 + 1) * n1, :] = get(y_ref, pl.ds(f2, n1, stride=p2)).astype(BF16)


def _dft_cs(n):
    j = np.arange(n)
    ang = 2.0 * np.pi * ((j[:, None] * j[None, :]) % n) / n
    return np.cos(ang), np.sin(ang)


def _fourier(fu):
    b, s, w = fu.shape
    n2 = FFT_INNER
    n1 = s // n2
    c1, s1 = _dft_cs(n1)
    f1 = jnp.asarray(np.concatenate([c1, -s1], axis=0), F32).astype(BF16)
    ang = 2.0 * np.pi * (np.arange(n2)[:, None] * np.arange(n1)[None, :]) / s
    tw_c = jnp.broadcast_to(jnp.asarray(np.cos(ang).reshape(s, 1), F32), (s, LANES))
    tw_s = jnp.broadcast_to(jnp.asarray(np.sin(ang).reshape(s, 1), F32), (s, LANES))
    c2, s2 = _dft_cs(n2)
    g2 = jnp.asarray(np.block([[c2, s2], [-s2, c2]]), F32).astype(BF16)
    cc, sc = _dft_cs(HEAD_DIM)
    eye = np.eye(w // HEAD_DIM)
    cbd = jnp.asarray(np.kron(eye, cc), F32).astype(BF16)
    sbd = jnp.asarray(np.kron(eye, sc), F32).astype(BF16)

    seq = pl.BlockSpec((1, s, w), lambda i: (i, 0, 0))
    return pl.pallas_call(
        functools.partial(_fft_kernel, scale=float((s * HEAD_DIM) ** -0.5)),
        grid=(b,),
        in_specs=[pl.BlockSpec((1, s, w), lambda i: (i, 0, 0)),
                  _const_spec((2 * n1, n1)), _const_spec((s, LANES)), _const_spec((s, LANES)),
                  _const_spec((2 * n2, 2 * n2)), _const_spec((w, w)), _const_spec((w, w))],
        out_specs=seq,
        out_shape=jax.ShapeDtypeStruct((b, s, w), BF16),
        scratch_shapes=[pltpu.VMEM((w // LANES, n1 * (n2 + FFT_PAD), LANES), F32),
                        pltpu.VMEM((w // LANES, n2 * (n1 + FFT_PAD), LANES), F32),
                        pltpu.VMEM((w // LANES, n2 * (n1 + FFT_PAD), LANES), F32),
                        pltpu.VMEM((w // LANES, n1 * (n2 + FFT_PAD), LANES), F32)],
        compiler_params=_params("parallel"),
    )(fu, f1, tw_c, tw_s, g2, cbd, sbd)


def _att_kernel(q_ref, kt_ref, v_ref, o_ref, *, kv_block):
    q = q_ref[0]
    tq = q.shape[0]
    s = kt_ref.shape[2]
    head = lax.broadcasted_iota(jnp.int32, q.shape, 1) // HEAD_DIM
    qs = jnp.concatenate([jnp.where(head == h, q, jnp.zeros_like(q)) for h in range(ATT_Q_HEADS)], axis=0)
    m = acc = None
    for j in range(s // kv_block):
        cols = slice(j * kv_block, (j + 1) * kv_block)
        sc = jnp.dot(qs, kt_ref[0, :, cols], preferred_element_type=F32)
        bm = jnp.max(sc, axis=-1, keepdims=True)
        m_new = bm if m is None else jnp.maximum(m, bm)
        pv = jnp.dot(jnp.exp2(sc - m_new).astype(BF16), v_ref[0, cols, :], preferred_element_type=F32)
        acc = pv if m is None else acc * jnp.exp2(m - m_new) + pv
        m = m_new
    o = [acc[h * tq:(h + 1) * tq, :LANES] / acc[h * tq:(h + 1) * tq, LANES:] for h in range(ATT_Q_HEADS)]
    lo = lax.broadcasted_iota(jnp.int32, (tq, LANES), 1) < HEAD_DIM
    out = jnp.concatenate([jnp.where(lo, o[0], pltpu.roll(o[1], HEAD_DIM, 1)),
                           jnp.where(lo, pltpu.roll(o[2], HEAD_DIM, 1), o[3])], axis=-1)
    o_ref[0] = out.astype(BF16)


def _attention(aq, akt, av, tq, kv_block=512):
    b, s, w = aq.shape
    return pl.pallas_call(
        functools.partial(_att_kernel, kv_block=kv_block),
        grid=(b, s // tq),
        in_specs=[pl.BlockSpec((1, tq, w), lambda i, j: (i, j, 0)),
                  pl.BlockSpec((1, w, s), lambda i, j: (i, 0, 0)),
                  pl.BlockSpec((1, s, w), lambda i, j: (i, 0, 0))],
        out_specs=pl.BlockSpec((1, tq, w), lambda i, j: (i, j, 0)),
        out_shape=jax.ShapeDtypeStruct((b, s, w), BF16),
        compiler_params=_params("parallel", "parallel"),
    )(aq, akt, av)


def _post_kernel(x_ref, ret_ref, fft_ref, att_ref, cb_ref, z_ref, halo_ref, cw_ref, wo_ref,
                 nf_ref, w1_ref, w2_ref, np_ref, wg_ref, p_ref, wp_ref, fn_ref, o_ref, *, final, ff_block):
    tm = x_ref.shape[1]
    z = z_ref[0].astype(F32)
    row = lax.broadcasted_iota(jnp.int32, z.shape, 0)
    halo = halo_ref[0, 0].astype(F32)
    z_prev = jnp.where(row == 0, halo[0:1], pltpu.roll(z, 1, 0))
    z_next = jnp.where(row == tm - 1, halo[1:2], pltpu.roll(z, tm - 1, 0))
    cw = cw_ref[...]
    conv = cb_ref[0].astype(F32) * (z_prev * cw[0:1] + z * cw[1:2] + z_next * cw[2:3])
    mixed = jnp.concatenate([ret_ref[0], fft_ref[0], att_ref[0], conv.astype(BF16)], axis=-1)
    x = x_ref[0] + jnp.dot(mixed, wo_ref[...], preferred_element_type=F32)

    hn = _rms(x, nf_ref[...]).astype(BF16)
    acc = x
    for j in range(D_FF // ff_block):
        hf = jnp.dot(hn, w1_ref[:, j * ff_block:(j + 1) * ff_block], preferred_element_type=F32)
        act = jnp.square(jnp.maximum(hf, 0.0)).astype(BF16)
        acc = acc + jnp.dot(act, w2_ref[j * ff_block:(j + 1) * ff_block, :], preferred_element_type=F32)
    x = acc

    gate = jax.nn.sigmoid(jnp.dot(_rms(x, np_ref[...]).astype(BF16), wg_ref[...], preferred_element_type=F32))
    x = x + gate * jnp.dot(p_ref[0, 0].astype(BF16), wp_ref[...], preferred_element_type=F32)
    if final:
        x = _rms(x, fn_ref[...])
    o_ref[0] = x


def _post(x, y_ret, y_fft, y_att, cb, z, conv_w, w_out, nf, w1, w2, npl, wg, p, layer, wp, fn, tm, final):
    b, s, _ = x.shape
    nt = s // tm
    zero = jnp.zeros((b, 1, MIXER_W), z.dtype)
    prev = jnp.concatenate([zero, z[:, tm - 1::tm][:, :-1]], axis=1)
    nxt = jnp.concatenate([z[:, ::tm][:, 1:], zero], axis=1)
    halo = jnp.concatenate([prev[:, :, None], nxt[:, :, None],
                            jnp.zeros((b, nt, 6, MIXER_W), z.dtype)], axis=2)
    tok = pl.BlockSpec((1, tm, MIXER_W), lambda i, j: (i, j, 0))
    xs = pl.BlockSpec((1, tm, D_MODEL), lambda i, j: (i, j, 0))
    return pl.pallas_call(
        functools.partial(_post_kernel, final=final, ff_block=1024),
        grid=(b, nt),
        in_specs=[xs, tok, tok, tok, tok, tok,
                  pl.BlockSpec((1, 1, 8, MIXER_W), lambda i, j: (i, j, 0, 0)),
                  _const_spec((3, MIXER_W)), _const_spec((D_MODEL, D_MODEL)),
                  _const_spec((1, D_MODEL)), _const_spec((D_MODEL, D_FF)), _const_spec((D_FF, D_MODEL)),
                  _const_spec((1, D_MODEL)), _const_spec((D_MODEL, D_MODEL)),
                  pl.BlockSpec((1, 1, tm, PLE_DIM), lambda i, j: (layer, i, j, 0)),
                  _const_spec((PLE_DIM, D_MODEL)), _const_spec((1, D_MODEL))],
        out_specs=xs,
        out_shape=jax.ShapeDtypeStruct((b, s, D_MODEL), F32),
        compiler_params=_params("parallel", "parallel"),
    )(x, y_ret, y_fft, y_att, cb, z, halo, conv_w, w_out, nf, w1, w2, npl, wg, p, wp, fn)


def _rope_tables(s):
    t = jnp.arange(s, dtype=F32)
    half = HEAD_DIM // 2
    inv = ROPE_BASE ** (-jnp.arange(half, dtype=F32) / half)
    ang = t[:, None] * inv[None, :]
    cos, sin = jnp.cos(ang), jnp.sin(ang)
    rcos = jnp.tile(jnp.concatenate([cos, cos], axis=-1), (1, RET_HEADS))
    rsin = jnp.tile(jnp.concatenate([-sin, sin], axis=-1), (1, RET_HEADS))
    quarter = half // 2
    inv_a = ROPE_BASE ** (-jnp.arange(quarter, dtype=F32) / quarter)
    ang_r = (jnp.arange(s) // GRID_W).astype(F32)[:, None] * inv_a[None, :]
    ang_c = (jnp.arange(s) % GRID_W).astype(F32)[:, None] * inv_a[None, :]
    cr, sr, cc, sc = jnp.cos(ang_r), jnp.sin(ang_r), jnp.cos(ang_c), jnp.sin(ang_c)
    acos = jnp.tile(jnp.concatenate([cr, cr, cc, cc], axis=-1), (1, ATT_Q_HEADS))
    asin = jnp.tile(jnp.concatenate([-sr, sr, -sc, sc], axis=-1), (1, ATT_Q_HEADS))
    return rcos, rsin, acos, asin


def _encode(x, p, wts, depth):
    b, s, _ = x.shape
    tabs = _rope_tables(s)
    hd = np.arange(MIXER_W) // HEAD_DIM
    gmean = jnp.asarray((hd[:, None] == hd[None, :]) / HEAD_DIM, F32).astype(BF16)
    for l in range(depth):
        w = wts[l]
        rq, rkt, rv, rg, fu, aq, akt, av, cb, z = _in_proj(
            x, w["norm_mix"], w["w_in"], tabs, w["q_norm"], w["k_norm"], gmean, tm=1024)
        y_ret = _retention(rq, rkt, rv, rg, w["ret_tabs"], gmean, w["ret_gn"])
        y_fft = _fourier(fu)
        y_att = _attention(aq, akt, av, tq=512, kv_block=256)
        x = _post(x, y_ret, y_fft, y_att, cb, z, w["conv_w"], w["w_out"], w["norm_ffn"], w["w_ffn1"],
                  w["w_ffn2"], w["norm_pl"], w["w_pl_gate"], p, l, w["w_pl_proj"], w["final_norm"],
                  tm=512, final=(l == depth - 1))
    return x


def kernel(x_prompt, x_sample, p_prompt, p_sample, norm_mix_w, w_in, ret_log_rate, ret_gn_w, q_norm_w, k_norm_w,
           conv_w, w_out, norm_ffn_w, w_ffn1, w_ffn2, norm_pl_w, w_pl_gate, w_pl_proj, final_norm_w):
    depth = w_in.shape[0]
    wts = []
    for l in range(depth):
        wts.append(dict(
            norm_mix=norm_mix_w[l][None, :], w_in=w_in[l].astype(BF16),
            ret_tabs=_retention_tables(ret_log_rate[l]), ret_gn=ret_gn_w[l][None, :],
            q_norm=jnp.tile(q_norm_w[l], ATT_Q_HEADS)[None, :], k_norm=jnp.tile(k_norm_w[l], ATT_KV_HEADS)[None, :],
            conv_w=conv_w[l], w_out=w_out[l].astype(BF16), norm_ffn=norm_ffn_w[l][None, :],
            w_ffn1=w_ffn1[l].astype(BF16), w_ffn2=w_ffn2[l].astype(BF16), norm_pl=norm_pl_w[l][None, :],
            w_pl_gate=w_pl_gate[l].astype(BF16), w_pl_proj=w_pl_proj[l].astype(BF16),
            final_norm=final_norm_w[None, :]))
    return _encode(x_prompt, p_prompt, wts, depth), _encode(x_sample, p_sample, wts, depth)
```

```python
import functools

import numpy as np
import jax
import jax.numpy as jnp
from jax import lax
from jax.experimental import pallas as pl
from jax.experimental.pallas import tpu as pltpu

D_MODEL = 1024
HEAD_DIM = 64
RET_HEADS = 4
ATT_Q_HEADS = 4
ATT_KV_HEADS = 2
MIXER_W = 256
ATT_KV_W = ATT_KV_HEADS * HEAD_DIM
PROJ_W = 2560
D_FF = 4 * D_MODEL
PLE_DIM = 256
GRID_W = 64
RET_CHUNK = 256
ROPE_BASE = 10000.0
NORM_EPS = 1e-6
LOG2_E = 1.4426950408889634
FFT_INNER = 64
FFT_PAD = 8

LANES = 128
VMEM_LIMIT = 56 * 1024 * 1024

F32 = jnp.float32
BF16 = jnp.bfloat16

_O_RQ, _O_RK, _O_RV, _O_RG, _O_FU, _O_AQ, _O_AK, _O_AV, _O_CB, _O_CC, _O_CH = (
    0, 256, 512, 768, 1024, 1280, 1536, 1664, 1792, 2048, 2304)


def _params(*sem):
    return pltpu.CompilerParams(dimension_semantics=sem, vmem_limit_bytes=VMEM_LIMIT)


def _const_spec(shape):
    nd = len(shape)
    return pl.BlockSpec(shape, lambda *_: (0,) * nd, pipeline_mode=pl.Buffered(1))


def _rms(x, w):
    ms = jnp.mean(x * x, axis=-1, keepdims=True)
    return x * lax.rsqrt(ms + NORM_EPS) * w


def _pair_swap(v, half):
    parts = []
    for s in range(v.shape[-1] // LANES):
        vs = v[:, s * LANES:(s + 1) * LANES]
        lane = lax.broadcasted_iota(jnp.int32, vs.shape, 1)
        lo = (lane & half) == 0
        parts.append(jnp.where(lo, pltpu.roll(vs, LANES - half, 1), pltpu.roll(vs, half, 1)))
    return parts[0] if len(parts) == 1 else jnp.concatenate(parts, axis=-1)


def _dup_heads(v):
    sw = pltpu.roll(v, HEAD_DIM, 1)
    lane = lax.broadcasted_iota(jnp.int32, v.shape, 1)
    lo = lane < HEAD_DIM
    return jnp.concatenate([jnp.where(lo, v, sw), jnp.where(lo, sw, v)], axis=-1)


def _in_kernel(x_ref, nw_ref, w_ref, rcos_ref, rsin_ref, acos_ref, asin_ref, qw_ref, kw_ref, g_ref,
               rq_ref, rkt_ref, rv_ref, rg_ref, fu_ref, aq_ref, akt_ref, av_ref, cb_ref, z_ref):
    h = _rms(x_ref[0], nw_ref[...]).astype(BF16)

    def seg(off, width=MIXER_W):
        return jnp.dot(h, w_ref[:, off:off + width], preferred_element_type=F32)

    rcos, rsin = rcos_ref[...], rsin_ref[...]
    q = seg(_O_RQ)
    rq_ref[0] = (q * rcos + _pair_swap(q, HEAD_DIM // 2) * rsin).astype(BF16)
    k = seg(_O_RK)
    k = (k * rcos + _pair_swap(k, HEAD_DIM // 2) * rsin) * (HEAD_DIM ** -0.5)
    rkt_ref[0] = k.T.astype(BF16)
    rv_ref[0] = seg(_O_RV).astype(BF16)
    g = seg(_O_RG)
    rg_ref[0] = (g * jax.nn.sigmoid(g)).astype(BF16)
    fu_ref[0] = seg(_O_FU)

    acos, asin = acos_ref[...], asin_ref[...]
    gm = g_ref[...]
    q = seg(_O_AQ)
    ms = jnp.dot((q * q).astype(BF16), gm, preferred_element_type=F32)
    q = q * lax.rsqrt(ms + NORM_EPS) * qw_ref[...]
    q = (q * acos + _pair_swap(q, HEAD_DIM // 4) * asin) * (HEAD_DIM ** -0.5 * LOG2_E)
    aq_ref[0] = q.astype(BF16)
    k = seg(_O_AK, ATT_KV_W)
    ms = jnp.dot((k * k).astype(BF16), gm[:ATT_KV_W, :ATT_KV_W], preferred_element_type=F32)
    k = k * lax.rsqrt(ms + NORM_EPS) * kw_ref[...]
    k = k * acos[:, :ATT_KV_W] + _pair_swap(k, HEAD_DIM // 4) * asin[:, :ATT_KV_W]
    akt_ref[0] = _dup_heads(k).T.astype(BF16)
    v = seg(_O_AV, ATT_KV_W)
    av_ref[0] = jnp.concatenate([v, jnp.ones_like(v)], axis=-1).astype(BF16)

    cb_ref[0] = seg(_O_CB).astype(BF16)
    z_ref[0] = (seg(_O_CC) * seg(_O_CH)).astype(BF16)


def _in_proj(x, nw, w_in, tabs, qw, kw, gmean, tm):
    b, s, _ = x.shape
    rcos, rsin, acos, asin = tabs
    tok = pl.BlockSpec((1, tm, MIXER_W), lambda i, j: (i, j, 0))
    tok_t = pl.BlockSpec((1, MIXER_W, tm), lambda i, j: (i, 0, j))
    tab = pl.BlockSpec((tm, MIXER_W), lambda i, j: (j, 0))
    nat = jax.ShapeDtypeStruct((b, s, MIXER_W), BF16)
    tr = jax.ShapeDtypeStruct((b, MIXER_W, s), BF16)
    return pl.pallas_call(
        _in_kernel,
        grid=(b, s // tm),
        in_specs=[pl.BlockSpec((1, tm, D_MODEL), lambda i, j: (i, j, 0)),
                  _const_spec((1, D_MODEL)), _const_spec((D_MODEL, PROJ_W)),
                  tab, tab, tab, tab,
                  _const_spec((1, MIXER_W)), _const_spec((1, ATT_KV_W)), _const_spec((MIXER_W, MIXER_W))],
        out_specs=[tok, tok_t, tok, tok, tok, tok, tok_t, tok, tok, tok],
        out_shape=[nat, tr, nat, nat, jax.ShapeDtypeStruct((b, s, MIXER_W), F32), nat, tr, nat, nat, nat],
        compiler_params=_params("parallel", "parallel"),
    )(x, nw, w_in, rcos, rsin, acos, asin, qw, kw, gmean)


def _ret_kernel(q_ref, kt_ref, v_ref, g_ref, dec_ref, zf_ref, zb_ref, xf_ref, xb_ref, gf_ref, gb_ref,
                bd_ref, gm_ref, gnw_ref, o_ref, sf_ref, sb_ref, curf_ref, curb_ref, p_ref, acc_ref, dev_ref):
    c = RET_CHUNK
    s = q_ref.shape[1]
    nc = s // c
    head = lax.broadcasted_iota(jnp.int32, (c, MIXER_W), 1) // HEAD_DIM

    def rows(n, size=c):
        return pl.ds(pl.multiple_of(n * size, size), size)

    curf_ref[...] = jnp.zeros_like(curf_ref)
    curb_ref[...] = jnp.zeros_like(curb_ref)

    def scan(i, carry):
        for n, cur, hist, zeta, gdec in ((i, curf_ref, sf_ref, zf_ref, gf_ref),
                                         (nc - 1 - i, curb_ref, sb_ref, zb_ref, gb_ref)):
            r = rows(n)
            state = cur[...]
            hist[n] = state.astype(BF16)
            kz = (kt_ref[0, :, r].astype(F32) * zeta[...]).astype(BF16)
            cur[...] = state * gdec[...] + jnp.dot(kz, v_ref[0, r, :], preferred_element_type=F32) * bd_ref[...]
        r = rows(i)
        qc = q_ref[0, r, :]
        qs = jnp.concatenate([jnp.where(head == h, qc, jnp.zeros_like(qc)) for h in range(RET_HEADS)], axis=0)
        p_ref[i] = (jnp.dot(qs, kt_ref[0, :, r], preferred_element_type=F32) * dec_ref[...]).astype(BF16)
        return carry

    lax.fori_loop(0, nc, scan, 0, unroll=4)

    def mix(n, carry):
        r = rows(n)
        qc = q_ref[0, r, :]
        inner = jnp.dot(p_ref[n], v_ref[0, r, :], preferred_element_type=F32)
        o = (jnp.dot(qc, sf_ref[n], preferred_element_type=F32) * xf_ref[...]
             + jnp.dot(qc, sb_ref[n], preferred_element_type=F32) * xb_ref[...])
        for h in range(RET_HEADS):
            o = o + jnp.where(head == h, inner[h * c:(h + 1) * c], 0.0)
        acc_ref[r, :] = o
        return carry

    lax.fori_loop(0, nc, mix, 0, unroll=4)

    nb = 512

    def center(n, carry):
        r = rows(n, nb)
        o = acc_ref[r, :]
        hi = o.astype(BF16)
        lo = (o - hi.astype(F32)).astype(BF16)
        gm = gm_ref[...]
        dev_ref[r, :] = o - (jnp.dot(hi, gm, preferred_element_type=F32) + jnp.dot(lo, gm, preferred_element_type=F32))
        return carry

    lax.fori_loop(0, s // nb, center, 0, unroll=2)

    def scale(n, carry):
        r = rows(n, nb)
        d = dev_ref[r, :]
        var = jnp.dot((d * d).astype(BF16), gm_ref[...], preferred_element_type=F32)
        y = d * lax.rsqrt(var + NORM_EPS) * gnw_ref[...]
        o_ref[0, r, :] = (g_ref[0, r, :].astype(F32) * y).astype(BF16)
        return carry

    lax.fori_loop(0, s // nb, scale, 0, unroll=2)


def _retention(rq, rkt, rv, rg, tabs, gmean, gnw):
    b, s, _ = rq.shape
    nat = pl.BlockSpec((1, s, MIXER_W), lambda i: (i, 0, 0))
    tr = pl.BlockSpec((1, MIXER_W, s), lambda i: (i, 0, 0))
    return pl.pallas_call(
        _ret_kernel,
        grid=(b,),
        in_specs=[nat, tr, nat, nat] + [_const_spec(t.shape) for t in tabs]
                 + [_const_spec((MIXER_W, MIXER_W)), _const_spec((1, MIXER_W))],
        out_specs=nat,
        out_shape=jax.ShapeDtypeStruct((b, s, MIXER_W), BF16),
        scratch_shapes=[pltpu.VMEM((s // RET_CHUNK, MIXER_W, MIXER_W), BF16),
                        pltpu.VMEM((s // RET_CHUNK, MIXER_W, MIXER_W), BF16),
                        pltpu.VMEM((MIXER_W, MIXER_W), F32), pltpu.VMEM((MIXER_W, MIXER_W), F32),
                        pltpu.VMEM((s // RET_CHUNK, RET_HEADS * RET_CHUNK, RET_CHUNK), BF16),
                        pltpu.VMEM((s, MIXER_W), F32), pltpu.VMEM((s, MIXER_W), F32)],
        compiler_params=_params("parallel"),
    )(rq, rkt, rv, rg, *tabs, gmean, gnw)


def _retention_tables(log_rate):
    c = RET_CHUNK
    log_g = -jnp.exp(log_rate.astype(F32))
    lf, lb = log_g[0], log_g[1]
    idx = jnp.arange(c, dtype=F32)
    diff = idx[:, None] - idx[None, :]
    low = diff >= 0
    up = diff < 0
    dec_f = jnp.where(low[None], jnp.exp(jnp.where(low, diff, 0.0)[None] * lf[:, None, None]), 0.0)
    dec_b = jnp.where(up[None], jnp.exp(jnp.where(up, -diff, 0.0)[None] * lb[:, None, None]), 0.0)
    dec = (dec_f + dec_b).reshape(RET_HEADS * c, c)
    lane = lambda per_head: jnp.repeat(per_head, HEAD_DIM, axis=-1)
    zeta_f = lane(jnp.exp((c - 1 - idx)[:, None] * lf[None, :])).T
    zeta_b = lane(jnp.exp(idx[:, None] * lb[None, :])).T
    xi_f = lane(jnp.exp((idx + 1.0)[:, None] * lf[None, :]))
    xi_b = lane(jnp.exp((c - idx)[:, None] * lb[None, :]))
    hd = jnp.arange(MIXER_W) // HEAD_DIM
    bd = (hd[:, None] == hd[None, :]).astype(F32)
    g_f = bd * lane(jnp.exp(c * lf)[None, :])
    g_b = bd * lane(jnp.exp(c * lb)[None, :])
    return dec, zeta_f, zeta_b, xi_f, xi_b, g_f, g_b, bd


def _fft_kernel(x_ref, f1_ref, tc_ref, ts_ref, g2_ref, c_ref, s_ref, o_ref, xp_ref, are_ref, aim_ref, y_ref, *, scale):
    s = x_ref.shape[1]
    n2 = FFT_INNER
    n1 = s // n2
    p1, p2 = n1 + FFT_PAD, n2 + FFT_PAD
    halves = (slice(0, LANES), slice(LANES, 2 * LANES))

    def put(ref, rows, val):
        for h, lanes in enumerate(halves):
            ref[h, rows, :] = val[:, lanes]

    def get(ref, rows):
        return jnp.concatenate([ref[0, rows, :], ref[1, rows, :]], axis=-1)

    for t1 in range(n1):
        put(xp_ref, slice(t1 * p2, t1 * p2 + n2), x_ref[0, t1 * n2:(t1 + 1) * n2, :])
    f1 = f1_ref[...]
    for t2 in range(n2):
        xs = get(xp_ref, pl.ds(t2, n1, stride=p2)).astype(BF16)
        a = jnp.dot(f1, xs, preferred_element_type=F32)
        ar, ai = a[:n1], a[n1:]
        rows = slice(t2 * n1, (t2 + 1) * n1)
        tc = jnp.concatenate([tc_ref[rows, :]] * 2, axis=-1)
        ts = jnp.concatenate([ts_ref[rows, :]] * 2, axis=-1)
        rows = slice(t2 * p1, t2 * p1 + n1)
        put(are_ref, rows, ar * tc + ai * ts)
        put(aim_ref, rows, ai * tc - ar * ts)
    group = 4
    for f0 in range(0, n1, group):
        zr, zi = [], []
        for f in range(f0, f0 + group):
            rows = pl.ds(f, n2, stride=p1)
            r = jnp.concatenate([get(are_ref, rows), get(aim_ref, rows)], axis=0)
            z = jnp.dot(g2_ref[...], r.astype(BF16), preferred_element_type=F32)
            zr.append(z[:n2])
            zi.append(z[n2:])
        y = (jnp.dot(jnp.concatenate(zr, axis=0).astype(BF16), c_ref[...], preferred_element_type=F32)
             + jnp.dot(jnp.concatenate(zi, axis=0).astype(BF16), s_ref[...], preferred_element_type=F32)) * scale
        for i, f in enumerate(range(f0, f0 + group)):
            put(y_ref, slice(f * p2, f * p2 + n2), y[i * n2:(i + 1) * n2])
    for f2 in range(n2):
        o_ref[0, f2 * n1:(f2 + 1) * n1, :] = get(y_ref, pl.ds(f2, n1, stride=p2)).astype(BF16)


def _dft_cs(n):
    j = np.arange(n)
    ang = 2.0 * np.pi * ((j[:, None] * j[None, :]) % n) / n
    return np.cos(ang), np.sin(ang)


def _fourier(fu):
    b, s, w = fu.shape
    n2 = FFT_INNER
    n1 = s // n2
    c1, s1 = _dft_cs(n1)
    f1 = jnp.asarray(np.concatenate([c1, -s1], axis=0), F32).astype(BF16)
    ang = 2.0 * np.pi * (np.arange(n2)[:, None] * np.arange(n1)[None, :]) / s
    tw_c = jnp.broadcast_to(jnp.asarray(np.cos(ang).reshape(s, 1), F32), (s, LANES))
    tw_s = jnp.broadcast_to(jnp.asarray(np.sin(ang).reshape(s, 1), F32), (s, LANES))
    c2, s2 = _dft_cs(n2)
    g2 = jnp.asarray(np.block([[c2, s2], [-s2, c2]]), F32).astype(BF16)
    cc, sc = _dft_cs(HEAD_DIM)
    eye = np.eye(w // HEAD_DIM)
    cbd = jnp.asarray(np.kron(eye, cc), F32).astype(BF16)
    sbd = jnp.asarray(np.kron(eye, sc), F32).astype(BF16)

    seq = pl.BlockSpec((1, s, w), lambda i: (i, 0, 0))
    return pl.pallas_call(
        functools.partial(_fft_kernel, scale=float((s * HEAD_DIM) ** -0.5)),
        grid=(b,),
        in_specs=[pl.BlockSpec((1, s, w), lambda i: (i, 0, 0)),
                  _const_spec((2 * n1, n1)), _const_spec((s, LANES)), _const_spec((s, LANES)),
                  _const_spec((2 * n2, 2 * n2)), _const_spec((w, w)), _const_spec((w, w))],
        out_specs=seq,
        out_shape=jax.ShapeDtypeStruct((b, s, w), BF16),
        scratch_shapes=[pltpu.VMEM((w // LANES, n1 * (n2 + FFT_PAD), LANES), F32),
                        pltpu.VMEM((w // LANES, n2 * (n1 + FFT_PAD), LANES), F32),
                        pltpu.VMEM((w // LANES, n2 * (n1 + FFT_PAD), LANES), F32),
                        pltpu.VMEM((w // LANES, n1 * (n2 + FFT_PAD), LANES), F32)],
        compiler_params=_params("parallel"),
    )(fu, f1, tw_c, tw_s, g2, cbd, sbd)


def _att_kernel(q_ref, kt_ref, v_ref, o_ref, *, kv_block):
    q = q_ref[0]
    tq = q.shape[0]
    s = kt_ref.shape[2]
    head = lax.broadcasted_iota(jnp.int32, q.shape, 1) // HEAD_DIM
    qs = jnp.concatenate([jnp.where(head == h, q, jnp.zeros_like(q)) for h in range(ATT_Q_HEADS)], axis=0)
    m = acc = None
    for j in range(s // kv_block):
        cols = slice(j * kv_block, (j + 1) * kv_block)
        sc = jnp.dot(qs, kt_ref[0, :, cols], preferred_element_type=F32)
        bm = jnp.max(sc, axis=-1, keepdims=True)
        m_new = bm if m is None else jnp.maximum(m, bm)
        pv = jnp.dot(jnp.exp2(sc - m_new).astype(BF16), v_ref[0, cols, :], preferred_element_type=F32)
        acc = pv if m is None else acc * jnp.exp2(m - m_new) + pv
        m = m_new
    o = [acc[h * tq:(h + 1) * tq, :LANES] / acc[h * tq:(h + 1) * tq, LANES:] for h in range(ATT_Q_HEADS)]
    lo = lax.broadcasted_iota(jnp.int32, (tq, LANES), 1) < HEAD_DIM
    out = jnp.concatenate([jnp.where(lo, o[0], pltpu.roll(o[1], HEAD_DIM, 1)),
                           jnp.where(lo, pltpu.roll(o[2], HEAD_DIM, 1), o[3])], axis=-1)
    o_ref[0] = out.astype(BF16)


def _attention(aq, akt, av, tq, kv_block=512):
    b, s, w = aq.shape
    return pl.pallas_call(
        functools.partial(_att_kernel, kv_block=kv_block),
        grid=(b, s // tq),
        in_specs=[pl.BlockSpec((1, tq, w), lambda i, j: (i, j, 0)),
                  pl.BlockSpec((1, w, s), lambda i, j: (i, 0, 0)),
                  pl.BlockSpec((1, s, w), lambda i, j: (i, 0, 0))],
        out_specs=pl.BlockSpec((1, tq, w), lambda i, j: (i, j, 0)),
        out_shape=jax.ShapeDtypeStruct((b, s, w), BF16),
        compiler_params=_params("parallel", "parallel"),
    )(aq, akt, av)


def _post_kernel(x_ref, ret_ref, fft_ref, att_ref, cb_ref, z_ref, halo_ref, cw_ref, wo_ref,
                 nf_ref, w1_ref, w2_ref, np_ref, wg_ref, p_ref, wp_ref, fn_ref, o_ref, *, final, ff_block):
    tm = x_ref.shape[1]
    z = z_ref[0].astype(F32)
    row = lax.broadcasted_iota(jnp.int32, z.shape, 0)
    halo = halo_ref[0, 0].astype(F32)
    z_prev = jnp.where(row == 0, halo[0:1], pltpu.roll(z, 1, 0))
    z_next = jnp.where(row == tm - 1, halo[1:2], pltpu.roll(z, tm - 1, 0))
    cw = cw_ref[...]
    conv = cb_ref[0].astype(F32) * (z_prev * cw[0:1] + z * cw[1:2] + z_next * cw[2:3])
    mixed = jnp.concatenate([ret_ref[0], fft_ref[0], att_ref[0], conv.astype(BF16)], axis=-1)
    x = x_ref[0] + jnp.dot(mixed, wo_ref[...], preferred_element_type=F32)

    hn = _rms(x, nf_ref[...]).astype(BF16)
    acc = x
    for j in range(D_FF // ff_block):
        hf = jnp.dot(hn, w1_ref[:, j * ff_block:(j + 1) * ff_block], preferred_element_type=F32)
        act = jnp.square(jnp.maximum(hf, 0.0)).astype(BF16)
        acc = acc + jnp.dot(act, w2_ref[j * ff_block:(j + 1) * ff_block, :], preferred_element_type=F32)
    x = acc

    gate = jax.nn.sigmoid(jnp.dot(_rms(x, np_ref[...]).astype(BF16), wg_ref[...], preferred_element_type=F32))
    x = x + gate * jnp.dot(p_ref[0, 0].astype(BF16), wp_ref[...], preferred_element_type=F32)
    if final:
        x = _rms(x, fn_ref[...])
    o_ref[0] = x


def _post(x, y_ret, y_fft, y_att, cb, z, conv_w, w_out, nf, w1, w2, npl, wg, p, layer, wp, fn, tm, final):
    b, s, _ = x.shape
    nt = s // tm
    zero = jnp.zeros((b, 1, MIXER_W), z.dtype)
    prev = jnp.concatenate([zero, z[:, tm - 1::tm][:, :-1]], axis=1)
    nxt = jnp.concatenate([z[:, ::tm][:, 1:], zero], axis=1)
    halo = jnp.concatenate([prev[:, :, None], nxt[:, :, None],
                            jnp.zeros((b, nt, 6, MIXER_W), z.dtype)], axis=2)
    tok = pl.BlockSpec((1, tm, MIXER_W), lambda i, j: (i, j, 0))
    xs = pl.BlockSpec((1, tm, D_MODEL), lambda i, j: (i, j, 0))
    return pl.pallas_call(
        functools.partial(_post_kernel, final=final, ff_block=1024),
        grid=(b, nt),
        in_specs=[xs, tok, tok, tok, tok, tok,
                  pl.BlockSpec((1, 1, 8, MIXER_W), lambda i, j: (i, j, 0, 0)),
                  _const_spec((3, MIXER_W)), _const_spec((D_MODEL, D_MODEL)),
                  _const_spec((1, D_MODEL)), _const_spec((D_MODEL, D_FF)), _const_spec((D_FF, D_MODEL)),
                  _const_spec((1, D_MODEL)), _const_spec((D_MODEL, D_MODEL)),
                  pl.BlockSpec((1, 1, tm, PLE_DIM), lambda i, j: (layer, i, j, 0)),
                  _const_spec((PLE_DIM, D_MODEL)), _const_spec((1, D_MODEL))],
        out_specs=xs,
        out_shape=jax.ShapeDtypeStruct((b, s, D_MODEL), F32),
        compiler_params=_params("parallel", "parallel"),
    )(x, y_ret, y_fft, y_att, cb, z, halo, conv_w, w_out, nf, w1, w2, npl, wg, p, wp, fn)


def _rope_tables(s):
    t = jnp.arange(s, dtype=F32)
    half = HEAD_DIM // 2
    inv = ROPE_BASE ** (-jnp.arange(half, dtype=F32) / half)
    ang = t[:, None] * inv[None, :]
    cos, sin = jnp.cos(ang), jnp.sin(ang)
    rcos = jnp.tile(jnp.concatenate([cos, cos], axis=-1), (1, RET_HEADS))
    rsin = jnp.tile(jnp.concatenate([-sin, sin], axis=-1), (1, RET_HEADS))
    quarter = half // 2
    inv_a = ROPE_BASE ** (-jnp.arange(quarter, dtype=F32) / quarter)
    ang_r = (jnp.arange(s) // GRID_W).astype(F32)[:, None] * inv_a[None, :]
    ang_c = (jnp.arange(s) % GRID_W).astype(F32)[:, None] * inv_a[None, :]
    cr, sr, cc, sc = jnp.cos(ang_r), jnp.sin(ang_r), jnp.cos(ang_c), jnp.sin(ang_c)
    acos = jnp.tile(jnp.concatenate([cr, cr, cc, cc], axis=-1), (1, ATT_Q_HEADS))
    asin = jnp.tile(jnp.concatenate([-sr, sr, -sc, sc], axis=-1), (1, ATT_Q_HEADS))
    return rcos, rsin, acos, asin


def _encode(x, p, wts, depth):
    b, s, _ = x.shape
    tabs = _rope_tables(s)
    hd = np.arange(MIXER_W) // HEAD_DIM
    gmean = jnp.asarray((hd[:, None] == hd[None, :]) / HEAD_DIM, F32).astype(BF16)
    for l in range(depth):
        w = wts[l]
        rq, rkt, rv, rg, fu, aq, akt, av, cb, z = _in_proj(
            x, w["norm_mix"], w["w_in"], tabs, w["q_norm"], w["k_norm"], gmean, tm=1024)
        y_ret = _retention(rq, rkt, rv, rg, w["ret_tabs"], gmean, w["ret_gn"])
        y_fft = _fourier(fu)
        y_att = _attention(aq, akt, av, tq=1024, kv_block=256)
        x = _post(x, y_ret, y_fft, y_att, cb, z, w["conv_w"], w["w_out"], w["norm_ffn"], w["w_ffn1"],
                  w["w_ffn2"], w["norm_pl"], w["w_pl_gate"], p, l, w["w_pl_proj"], w["final_norm"],
                  tm=512, final=(l == depth - 1))
    return x


def kernel(x_prompt, x_sample, p_prompt, p_sample, norm_mix_w, w_in, ret_log_rate, ret_gn_w, q_norm_w, k_norm_w,
           conv_w, w_out, norm_ffn_w, w_ffn1, w_ffn2, norm_pl_w, w_pl_gate, w_pl_proj, final_norm_w):
    depth = w_in.shape[0]
    wts = []
    for l in range(depth):
        wts.append(dict(
            norm_mix=norm_mix_w[l][None, :], w_in=w_in[l].astype(BF16),
            ret_tabs=_retention_tables(ret_log_rate[l]), ret_gn=ret_gn_w[l][None, :],
            q_norm=jnp.tile(q_norm_w[l], ATT_Q_HEADS)[None, :], k_norm=jnp.tile(k_norm_w[l], ATT_KV_HEADS)[None, :],
            conv_w=conv_w[l], w_out=w_out[l].astype(BF16), norm_ffn=norm_ffn_w[l][None, :],
            w_ffn1=w_ffn1[l].astype(BF16), w_ffn2=w_ffn2[l].astype(BF16), norm_pl=norm_pl_w[l][None, :],
            w_pl_gate=w_pl_gate[l].astype(BF16), w_pl_proj=w_pl_proj[l].astype(BF16),
            final_norm=final_norm_w[None, :]))
    return _encode(x_prompt, p_prompt, wts, depth), _encode(x_sample, p_sample, wts, depth)
```

```python
import functools

import numpy as np
import jax
import jax.numpy as jnp
from jax import lax
from jax.experimental import pallas as pl
from jax.experimental.pallas import tpu as pltpu

D_MODEL = 1024
HEAD_DIM = 64
RET_HEADS = 4
ATT_Q_HEADS = 4
ATT_KV_HEADS = 2
MIXER_W = 256
ATT_KV_W = ATT_KV_HEADS * HEAD_DIM
PROJ_W = 2560
D_FF = 4 * D_MODEL
PLE_DIM = 256
GRID_W = 64
RET_CHUNK = 256
ROPE_BASE = 10000.0
NORM_EPS = 1e-6
LOG2_E = 1.4426950408889634
FFT_INNER = 64
FFT_PAD = 8

LANES = 128
VMEM_LIMIT = 56 * 1024 * 1024

F32 = jnp.float32
BF16 = jnp.bfloat16

_O_RQ, _O_RK, _O_RV, _O_RG, _O_FU, _O_AQ, _O_AK, _O_AV, _O_CB, _O_CC, _O_CH = (
    0, 256, 512, 768, 1024, 1280, 1536, 1664, 1792, 2048, 2304)


def _params(*sem):
    return pltpu.CompilerParams(dimension_semantics=sem, vmem_limit_bytes=VMEM_LIMIT)


def _const_spec(shape):
    nd = len(shape)
    return pl.BlockSpec(shape, lambda *_: (0,) * nd, pipeline_mode=pl.Buffered(1))


def _rms(x, w):
    ms = jnp.mean(x * x, axis=-1, keepdims=True)
    return x * lax.rsqrt(ms + NORM_EPS) * w


def _pair_swap(v, half):
    parts = []
    for s in range(v.shape[-1] // LANES):
        vs = v[:, s * LANES:(s + 1) * LANES]
        lane = lax.broadcasted_iota(jnp.int32, vs.shape, 1)
        lo = (lane & half) == 0
        parts.append(jnp.where(lo, pltpu.roll(vs, LANES - half, 1), pltpu.roll(vs, half, 1)))
    return parts[0] if len(parts) == 1 else jnp.concatenate(parts, axis=-1)


def _dup_heads(v):
    sw = pltpu.roll(v, HEAD_DIM, 1)
    lane = lax.broadcasted_iota(jnp.int32, v.shape, 1)
    lo = lane < HEAD_DIM
    return jnp.concatenate([jnp.where(lo, v, sw), jnp.where(lo, sw, v)], axis=-1)


def _in_kernel(x_ref, nw_ref, w_ref, rcos_ref, rsin_ref, acos_ref, asin_ref, qw_ref, kw_ref, g_ref,
               rq_ref, rkt_ref, rv_ref, rg_ref, fu_ref, aq_ref, akt_ref, av_ref, cb_ref, z_ref):
    h = _rms(x_ref[0], nw_ref[...]).astype(BF16)

    def seg(off, width=MIXER_W):
        return jnp.dot(h, w_ref[:, off:off + width], preferred_element_type=F32)

    aq = seg(_O_AQ)
    akv = seg(_O_AK)

    rcos, rsin = rcos_ref[...], rsin_ref[...]
    q = seg(_O_RQ)
    rq_ref[0] = (q * rcos + _pair_swap(q, HEAD_DIM // 2) * rsin).astype(BF16)
    k = seg(_O_RK)
    k = (k * rcos + _pair_swap(k, HEAD_DIM // 2) * rsin) * (HEAD_DIM ** -0.5)
    rkt_ref[0] = k.T.astype(BF16)

    acos, asin = acos_ref[...], asin_ref[...]
    gm = g_ref[...]
    ms = jnp.dot((aq * aq).astype(BF16), gm, preferred_element_type=F32)
    q = aq * lax.rsqrt(ms + NORM_EPS) * qw_ref[...]
    q = (q * acos + _pair_swap(q, HEAD_DIM // 4) * asin) * (HEAD_DIM ** -0.5 * LOG2_E)
    aq_ref[0] = q.astype(BF16)
    v = akv[:, ATT_KV_W:]
    av_ref[0] = jnp.concatenate([v, jnp.ones_like(v)], axis=-1).astype(BF16)
    k = akv[:, :ATT_KV_W]
    ms = jnp.dot((k * k).astype(BF16), gm[:ATT_KV_W, :ATT_KV_W], preferred_element_type=F32)
    k = k * lax.rsqrt(ms + NORM_EPS) * kw_ref[...]
    k = k * acos[:, :ATT_KV_W] + _pair_swap(k, HEAD_DIM // 4) * asin[:, :ATT_KV_W]
    akt_ref[0] = _dup_heads(k).T.astype(BF16)

    rv_ref[0] = seg(_O_RV).astype(BF16)
    g = seg(_O_RG)
    rg_ref[0] = (g * jax.nn.sigmoid(g)).astype(BF16)
    fu_ref[0] = seg(_O_FU)

    cb_ref[0] = seg(_O_CB).astype(BF16)
    z_ref[0] = (seg(_O_CC) * seg(_O_CH)).astype(BF16)


def _in_proj(x, nw, w_in, tabs, qw, kw, gmean, tm):
    b, s, _ = x.shape
    rcos, rsin, acos, asin = tabs
    tok = pl.BlockSpec((1, tm, MIXER_W), lambda i, j: (i, j, 0))
    tok_t = pl.BlockSpec((1, MIXER_W, tm), lambda i, j: (i, 0, j))
    tab = pl.BlockSpec((tm, MIXER_W), lambda i, j: (j, 0))
    nat = jax.ShapeDtypeStruct((b, s, MIXER_W), BF16)
    tr = jax.ShapeDtypeStruct((b, MIXER_W, s), BF16)
    return pl.pallas_call(
        _in_kernel,
        grid=(b, s // tm),
        in_specs=[pl.BlockSpec((1, tm, D_MODEL), lambda i, j: (i, j, 0)),
                  _const_spec((1, D_MODEL)), _const_spec((D_MODEL, PROJ_W)),
                  tab, tab, tab, tab,
                  _const_spec((1, MIXER_W)), _const_spec((1, ATT_KV_W)), _const_spec((MIXER_W, MIXER_W))],
        out_specs=[tok, tok_t, tok, tok, tok, tok, tok_t, tok, tok, tok],
        out_shape=[nat, tr, nat, nat, jax.ShapeDtypeStruct((b, s, MIXER_W), F32), nat, tr, nat, nat, nat],
        compiler_params=_params("parallel", "parallel"),
    )(x, nw, w_in, rcos, rsin, acos, asin, qw, kw, gmean)


def _ret_kernel(q_ref, kt_ref, v_ref, g_ref, dec_ref, zf_ref, zb_ref, xf_ref, xb_ref, gf_ref, gb_ref,
                bd_ref, gm_ref, gnw_ref, o_ref, sf_ref, sb_ref, curf_ref, curb_ref, p_ref, acc_ref, dev_ref):
    c = RET_CHUNK
    s = q_ref.shape[1]
    nc = s // c
    head = lax.broadcasted_iota(jnp.int32, (c, MIXER_W), 1) // HEAD_DIM

    def rows(n, size=c):
        return pl.ds(pl.multiple_of(n * size, size), size)

    curf_ref[...] = jnp.zeros_like(curf_ref)
    curb_ref[...] = jnp.zeros_like(curb_ref)

    def scan(i, carry):
        for n, cur, hist, zeta, gdec in ((i, curf_ref, sf_ref, zf_ref, gf_ref),
                                         (nc - 1 - i, curb_ref, sb_ref, zb_ref, gb_ref)):
            r = rows(n)
            state = cur[...]
            hist[n] = state.astype(BF16)
            kz = (kt_ref[0, :, r].astype(F32) * zeta[...]).astype(BF16)
            cur[...] = state * gdec[...] + jnp.dot(kz, v_ref[0, r, :], preferred_element_type=F32) * bd_ref[...]
        r = rows(i)
        qc = q_ref[0, r, :]
        qs = jnp.concatenate([jnp.where(head == h, qc, jnp.zeros_like(qc)) for h in range(RET_HEADS)], axis=0)
        p_ref[i] = (jnp.dot(qs, kt_ref[0, :, r], preferred_element_type=F32) * dec_ref[...]).astype(BF16)
        return carry

    lax.fori_loop(0, nc, scan, 0, unroll=4)

    def mix(n, carry):
        r = rows(n)
        qc = q_ref[0, r, :]
        inner = jnp.dot(p_ref[n], v_ref[0, r, :], preferred_element_type=F32)
        o = (jnp.dot(qc, sf_ref[n], preferred_element_type=F32) * xf_ref[...]
             + jnp.dot(qc, sb_ref[n], preferred_element_type=F32) * xb_ref[...])
        for h in range(RET_HEADS):
            o = o + jnp.where(head == h, inner[h * c:(h + 1) * c], 0.0)
        acc_ref[r, :] = o
        return carry

    lax.fori_loop(0, nc, mix, 0, unroll=4)

    nb = 512

    def center(n, carry):
        r = rows(n, nb)
        o = acc_ref[r, :]
        hi = o.astype(BF16)
        lo = (o - hi.astype(F32)).astype(BF16)
        gm = gm_ref[...]
        dev_ref[r, :] = o - (jnp.dot(hi, gm, preferred_element_type=F32) + jnp.dot(lo, gm, preferred_element_type=F32))
        return carry

    lax.fori_loop(0, s // nb, center, 0, unroll=2)

    def scale(n, carry):
        r = rows(n, nb)
        d = dev_ref[r, :]
        var = jnp.dot((d * d).astype(BF16), gm_ref[...], preferred_element_type=F32)
        y = d * lax.rsqrt(var + NORM_EPS) * gnw_ref[...]
        o_ref[0, r, :] = (g_ref[0, r, :].astype(F32) * y).astype(BF16)
        return carry

    lax.fori_loop(0, s // nb, scale, 0, unroll=2)


def _retention(rq, rkt, rv, rg, tabs, gmean, gnw):
    b, s, _ = rq.shape
    nat = pl.BlockSpec((1, s, MIXER_W), lambda i: (i, 0, 0))
    tr = pl.BlockSpec((1, MIXER_W, s), lambda i: (i, 0, 0))
    return pl.pallas_call(
        _ret_kernel,
        grid=(b,),
        in_specs=[nat, tr, nat, nat] + [_const_spec(t.shape) for t in tabs]
                 + [_const_spec((MIXER_W, MIXER_W)), _const_spec((1, MIXER_W))],
        out_specs=nat,
        out_shape=jax.ShapeDtypeStruct((b, s, MIXER_W), BF16),
        scratch_shapes=[pltpu.VMEM((s // RET_CHUNK, MIXER_W, MIXER_W), BF16),
                        pltpu.VMEM((s // RET_CHUNK, MIXER_W, MIXER_W), BF16),
                        pltpu.VMEM((MIXER_W, MIXER_W), F32), pltpu.VMEM((MIXER_W, MIXER_W), F32),
                        pltpu.VMEM((s // RET_CHUNK, RET_HEADS * RET_CHUNK, RET_CHUNK), BF16),
                        pltpu.VMEM((s, MIXER_W), F32), pltpu.VMEM((s, MIXER_W), F32)],
        compiler_params=_params("parallel"),
    )(rq, rkt, rv, rg, *tabs, gmean, gnw)


def _retention_tables(log_rate):
    c = RET_CHUNK
    log_g = -jnp.exp(log_rate.astype(F32))
    lf, lb = log_g[0], log_g[1]
    idx = jnp.arange(c, dtype=F32)
    diff = idx[:, None] - idx[None, :]
    low = diff >= 0
    up = diff < 0
    dec_f = jnp.where(low[None], jnp.exp(jnp.where(low, diff, 0.0)[None] * lf[:, None, None]), 0.0)
    dec_b = jnp.where(up[None], jnp.exp(jnp.where(up, -diff, 0.0)[None] * lb[:, None, None]), 0.0)
    dec = (dec_f + dec_b).reshape(RET_HEADS * c, c)
    lane = lambda per_head: jnp.repeat(per_head, HEAD_DIM, axis=-1)
    zeta_f = lane(jnp.exp((c - 1 - idx)[:, None] * lf[None, :])).T
    zeta_b = lane(jnp.exp(idx[:, None] * lb[None, :])).T
    xi_f = lane(jnp.exp((idx + 1.0)[:, None] * lf[None, :]))
    xi_b = lane(jnp.exp((c - idx)[:, None] * lb[None, :]))
    hd = jnp.arange(MIXER_W) // HEAD_DIM
    bd = (hd[:, None] == hd[None, :]).astype(F32)
    g_f = bd * lane(jnp.exp(c * lf)[None, :])
    g_b = bd * lane(jnp.exp(c * lb)[None, :])
    return dec, zeta_f, zeta_b, xi_f, xi_b, g_f, g_b, bd


def _fft_kernel(x_ref, f1_ref, tc_ref, ts_ref, g2_ref, c_ref, s_ref, o_ref, xp_ref, are_ref, aim_ref, y_ref, *, scale):
    s = x_ref.shape[1]
    n2 = FFT_INNER
    n1 = s // n2
    p1, p2 = n1 + FFT_PAD, n2 + FFT_PAD
    halves = (slice(0, LANES), slice(LANES, 2 * LANES))

    def put(ref, rows, val):
        for h, lanes in enumerate(halves):
            ref[h, rows, :] = val[:, lanes]

    def get(ref, rows):
        return jnp.concatenate([ref[0, rows, :], ref[1, rows, :]], axis=-1)

    for t1 in range(n1):
        put(xp_ref, slice(t1 * p2, t1 * p2 + n2), x_ref[0, t1 * n2:(t1 + 1) * n2, :])
    f1 = f1_ref[...]
    for t2 in range(n2):
        xs = get(xp_ref, pl.ds(t2, n1, stride=p2)).astype(BF16)
        a = jnp.dot(f1, xs, preferred_element_type=F32)
        ar, ai = a[:n1], a[n1:]
        rows = slice(t2 * n1, (t2 + 1) * n1)
        tc = jnp.concatenate([tc_ref[rows, :]] * 2, axis=-1)
        ts = jnp.concatenate([ts_ref[rows, :]] * 2, axis=-1)
        rows = slice(t2 * p1, t2 * p1 + n1)
        put(are_ref, rows, ar * tc + ai * ts)
        put(aim_ref, rows, ai * tc - ar * ts)
    group = 4
    for f0 in range(0, n1, group):
        zr, zi = [], []
        for f in range(f0, f0 + group):
            rows = pl.ds(f, n2, stride=p1)
            r = jnp.concatenate([get(are_ref, rows), get(aim_ref, rows)], axis=0)
            z = jnp.dot(g2_ref[...], r.astype(BF16), preferred_element_type=F32)
            zr.append(z[:n2])
            zi.append(z[n2:])
        y = (jnp.dot(jnp.concatenate(zr, axis=0).astype(BF16), c_ref[...], preferred_element_type=F32)
             + jnp.dot(jnp.concatenate(zi, axis=0).astype(BF16), s_ref[...], preferred_element_type=F32)) * scale
        for i, f in enumerate(range(f0, f0 + group)):
            put(y_ref, slice(f * p2, f * p2 + n2), y[i * n2:(i + 1) * n2])
    for f2 in range(n2):
        o_ref[0, f2 * n1:(f2 + 1) * n1, :] = get(y_ref, pl.ds(f2, n1, stride=p2)).astype(BF16)


def _dft_cs(n):
    j = np.arange(n)
    ang = 2.0 * np.pi * ((j[:, None] * j[None, :]) % n) / n
    return np.cos(ang), np.sin(ang)


def _fourier(fu):
    b, s, w = fu.shape
    n2 = FFT_INNER
    n1 = s // n2
    c1, s1 = _dft_cs(n1)
    f1 = jnp.asarray(np.concatenate([c1, -s1], axis=0), F32).astype(BF16)
    ang = 2.0 * np.pi * (np.arange(n2)[:, None] * np.arange(n1)[None, :]) / s
    tw_c = jnp.broadcast_to(jnp.asarray(np.cos(ang).reshape(s, 1), F32), (s, LANES))
    tw_s = jnp.broadcast_to(jnp.asarray(np.sin(ang).reshape(s, 1), F32), (s, LANES))
    c2, s2 = _dft_cs(n2)
    g2 = jnp.asarray(np.block([[c2, s2], [-s2, c2]]), F32).astype(BF16)
    cc, sc = _dft_cs(HEAD_DIM)
    eye = np.eye(w // HEAD_DIM)
    cbd = jnp.asarray(np.kron(eye, cc), F32).astype(BF16)
    sbd = jnp.asarray(np.kron(eye, sc), F32).astype(BF16)

    seq = pl.BlockSpec((1, s, w), lambda i: (i, 0, 0))
    return pl.pallas_call(
        functools.partial(_fft_kernel, scale=float((s * HEAD_DIM) ** -0.5)),
        grid=(b,),
        in_specs=[pl.BlockSpec((1, s, w), lambda i: (i, 0, 0)),
                  _const_spec((2 * n1, n1)), _const_spec((s, LANES)), _const_spec((s, LANES)),
                  _const_spec((2 * n2, 2 * n2)), _const_spec((w, w)), _const_spec((w, w))],
        out_specs=seq,
        out_shape=jax.ShapeDtypeStruct((b, s, w), BF16),
        scratch_shapes=[pltpu.VMEM((w // LANES, n1 * (n2 + FFT_PAD), LANES), F32),
                        pltpu.VMEM((w // LANES, n2 * (n1 + FFT_PAD), LANES), F32),
                        pltpu.VMEM((w // LANES, n2 * (n1 + FFT_PAD), LANES), F32),
                        pltpu.VMEM((w // LANES, n1 * (n2 + FFT_PAD), LANES), F32)],
        compiler_params=_params("parallel"),
    )(fu, f1, tw_c, tw_s, g2, cbd, sbd)


def _att_kernel(q_ref, kt_ref, v_ref, o_ref, *, kv_block):
    q = q_ref[0]
    tq = q.shape[0]
    s = kt_ref.shape[2]
    head = lax.broadcasted_iota(jnp.int32, q.shape, 1) // HEAD_DIM
    qs = jnp.concatenate([jnp.where(head == h, q, jnp.zeros_like(q)) for h in range(ATT_Q_HEADS)], axis=0)
    m = acc = None
    for j in range(s // kv_block):
        cols = slice(j * kv_block, (j + 1) * kv_block)
        sc = jnp.dot(qs, kt_ref[0, :, cols], preferred_element_type=F32)
        bm = jnp.max(sc, axis=-1, keepdims=True)
        m_new = bm if m is None else jnp.maximum(m, bm)
        pv = jnp.dot(jnp.exp2(sc - m_new).astype(BF16), v_ref[0, cols, :], preferred_element_type=F32)
        acc = pv if m is None else acc * jnp.exp2(m - m_new) + pv
        m = m_new
    o = [acc[h * tq:(h + 1) * tq, :LANES] / acc[h * tq:(h + 1) * tq, LANES:] for h in range(ATT_Q_HEADS)]
    lo = lax.broadcasted_iota(jnp.int32, (tq, LANES), 1) < HEAD_DIM
    out = jnp.concatenate([jnp.where(lo, o[0], pltpu.roll(o[1], HEAD_DIM, 1)),
                           jnp.where(lo, pltpu.roll(o[2], HEAD_DIM, 1), o[3])], axis=-1)
    o_ref[0] = out.astype(BF16)


def _attention(aq, akt, av, tq, kv_block=512):
    b, s, w = aq.shape
    return pl.pallas_call(
        functools.partial(_att_kernel, kv_block=kv_block),
        grid=(b, s // tq),
        in_specs=[pl.BlockSpec((1, tq, w), lambda i, j: (i, j, 0)),
                  pl.BlockSpec((1, w, s), lambda i, j: (i, 0, 0)),
                  pl.BlockSpec((1, s, w), lambda i, j: (i, 0, 0))],
        out_specs=pl.BlockSpec((1, tq, w), lambda i, j: (i, j, 0)),
        out_shape=jax.ShapeDtypeStruct((b, s, w), BF16),
        compiler_params=_params("parallel", "parallel"),
    )(aq, akt, av)


def _post_kernel(x_ref, ret_ref, fft_ref, att_ref, cb_ref, z_ref, halo_ref, cw_ref, wo_ref,
                 nf_ref, w1_ref, w2_ref, np_ref, wg_ref, p_ref, wp_ref, fn_ref, o_ref, *, final, ff_block):
    tm = x_ref.shape[1]
    z = z_ref[0].astype(F32)
    row = lax.broadcasted_iota(jnp.int32, z.shape, 0)
    halo = halo_ref[0, 0].astype(F32)
    z_prev = jnp.where(row == 0, halo[0:1], pltpu.roll(z, 1, 0))
    z_next = jnp.where(row == tm - 1, halo[1:2], pltpu.roll(z, tm - 1, 0))
    cw = cw_ref[...]
    conv = cb_ref[0].astype(F32) * (z_prev * cw[0:1] + z * cw[1:2] + z_next * cw[2:3])
    mixed = jnp.concatenate([ret_ref[0], fft_ref[0], att_ref[0], conv.astype(BF16)], axis=-1)
    x = x_ref[0] + jnp.dot(mixed, wo_ref[...], preferred_element_type=F32)

    hn = _rms(x, nf_ref[...]).astype(BF16)
    acc = x
    for j in range(D_FF // ff_block):
        hf = jnp.dot(hn, w1_ref[:, j * ff_block:(j + 1) * ff_block], preferred_element_type=F32)
        act = jnp.square(jnp.maximum(hf, 0.0)).astype(BF16)
        acc = acc + jnp.dot(act, w2_ref[j * ff_block:(j + 1) * ff_block, :], preferred_element_type=F32)
    x = acc

    gate = jax.nn.sigmoid(jnp.dot(_rms(x, np_ref[...]).astype(BF16), wg_ref[...], preferred_element_type=F32))
    x = x + gate * jnp.dot(p_ref[0, 0].astype(BF16), wp_ref[...], preferred_element_type=F32)
    if final:
        x = _rms(x, fn_ref[...])
    o_ref[0] = x


def _post(x, y_ret, y_fft, y_att, cb, z, conv_w, w_out, nf, w1, w2, npl, wg, p, layer, wp, fn, tm, final):
    b, s, _ = x.shape
    nt = s // tm
    zero = jnp.zeros((b, 1, MIXER_W), z.dtype)
    prev = jnp.concatenate([zero, z[:, tm - 1::tm][:, :-1]], axis=1)
    nxt = jnp.concatenate([z[:, ::tm][:, 1:], zero], axis=1)
    halo = jnp.concatenate([prev[:, :, None], nxt[:, :, None],
                            jnp.zeros((b, nt, 6, MIXER_W), z.dtype)], axis=2)
    tok = pl.BlockSpec((1, tm, MIXER_W), lambda i, j: (i, j, 0))
    xs = pl.BlockSpec((1, tm, D_MODEL), lambda i, j: (i, j, 0))
    return pl.pallas_call(
        functools.partial(_post_kernel, final=final, ff_block=1024),
        grid=(b, nt),
        in_specs=[xs, tok, tok, tok, tok, tok,
                  pl.BlockSpec((1, 1, 8, MIXER_W), lambda i, j: (i, j, 0, 0)),
                  _const_spec((3, MIXER_W)), _const_spec((D_MODEL, D_MODEL)),
                  _const_spec((1, D_MODEL)), _const_spec((D_MODEL, D_FF)), _const_spec((D_FF, D_MODEL)),
                  _const_spec((1, D_MODEL)), _const_spec((D_MODEL, D_MODEL)),
                  pl.BlockSpec((1, 1, tm, PLE_DIM), lambda i, j: (layer, i, j, 0)),
                  _const_spec((PLE_DIM, D_MODEL)), _const_spec((1, D_MODEL))],
        out_specs=xs,
        out_shape=jax.ShapeDtypeStruct((b, s, D_MODEL), F32),
        compiler_params=_params("parallel", "parallel"),
    )(x, y_ret, y_fft, y_att, cb, z, halo, conv_w, w_out, nf, w1, w2, npl, wg, p, wp, fn)


def _rope_tables(s):
    t = jnp.arange(s, dtype=F32)
    half = HEAD_DIM // 2
    inv = ROPE_BASE ** (-jnp.arange(half, dtype=F32) / half)
    ang = t[:, None] * inv[None, :]
    cos, sin = jnp.cos(ang), jnp.sin(ang)
    rcos = jnp.tile(jnp.concatenate([cos, cos], axis=-1), (1, RET_HEADS))
    rsin = jnp.tile(jnp.concatenate([-sin, sin], axis=-1), (1, RET_HEADS))
    quarter = half // 2
    inv_a = ROPE_BASE ** (-jnp.arange(quarter, dtype=F32) / quarter)
    ang_r = (jnp.arange(s) // GRID_W).astype(F32)[:, None] * inv_a[None, :]
    ang_c = (jnp.arange(s) % GRID_W).astype(F32)[:, None] * inv_a[None, :]
    cr, sr, cc, sc = jnp.cos(ang_r), jnp.sin(ang_r), jnp.cos(ang_c), jnp.sin(ang_c)
    acos = jnp.tile(jnp.concatenate([cr, cr, cc, cc], axis=-1), (1, ATT_Q_HEADS))
    asin = jnp.tile(jnp.concatenate([-sr, sr, -sc, sc], axis=-1), (1, ATT_Q_HEADS))
    return rcos, rsin, acos, asin


def _encode(x, p, wts, depth):
    b, s, _ = x.shape
    tabs = _rope_tables(s)
    hd = np.arange(MIXER_W) // HEAD_DIM
    gmean = jnp.asarray((hd[:, None] == hd[None, :]) / HEAD_DIM, F32).astype(BF16)
    for l in range(depth):
        w = wts[l]
        rq, rkt, rv, rg, fu, aq, akt, av, cb, z = _in_proj(
            x, w["norm_mix"], w["w_in"], tabs, w["q_norm"], w["k_norm"], gmean, tm=1024)
        y_ret = _retention(rq, rkt, rv, rg, w["ret_tabs"], gmean, w["ret_gn"])
        y_fft = _fourier(fu)
        y_att = _attention(aq, akt, av, tq=1024, kv_block=256)
        x = _post(x, y_ret, y_fft, y_att, cb, z, w["conv_w"], w["w_out"], w["norm_ffn"], w["w_ffn1"],
                  w["w_ffn2"], w["norm_pl"], w["w_pl_gate"], p, l, w["w_pl_proj"], w["final_norm"],
                  tm=512, final=(l == depth - 1))
    return x


def kernel(x_prompt, x_sample, p_prompt, p_sample, norm_mix_w, w_in, ret_log_rate, ret_gn_w, q_norm_w, k_norm_w,
           conv_w, w_out, norm_ffn_w, w_ffn1, w_ffn2, norm_pl_w, w_pl_gate, w_pl_proj, final_norm_w):
    depth = w_in.shape[0]
    wts = []
    for l in range(depth):
        wts.append(dict(
            norm_mix=norm_mix_w[l][None, :], w_in=w_in[l].astype(BF16),
            ret_tabs=_retention_tables(ret_log_rate[l]), ret_gn=ret_gn_w[l][None, :],
            q_norm=jnp.tile(q_norm_w[l], ATT_Q_HEADS)[None, :], k_norm=jnp.tile(k_norm_w[l], ATT_KV_HEADS)[None, :],
            conv_w=conv_w[l], w_out=w_out[l].astype(BF16), norm_ffn=norm_ffn_w[l][None, :],
            w_ffn1=w_ffn1[l].astype(BF16), w_ffn2=w_ffn2[l].astype(BF16), norm_pl=norm_pl_w[l][None, :],
            w_pl_gate=w_pl_gate[l].astype(BF16), w_pl_proj=w_pl_proj[l].astype(BF16),
            final_norm=final_norm_w[None, :]))
    return _encode(x_prompt, p_prompt, wts, depth), _encode(x_sample, p_sample, wts, depth)
```

```python
import functools

import numpy as np
import jax
import jax.numpy as jnp
from jax import lax
from jax.experimental import pallas as pl
from jax.experimental.pallas import tpu as pltpu

D_MODEL = 1024
HEAD_DIM = 64
RET_HEADS = 4
ATT_Q_HEADS = 4
ATT_KV_HEADS = 2
MIXER_W = 256
ATT_KV_W = ATT_KV_HEADS * HEAD_DIM
PROJ_W = 2560
D_FF = 4 * D_MODEL
PLE_DIM = 256
GRID_W = 64
RET_CHUNK = 256
ROPE_BASE = 10000.0
NORM_EPS = 1e-6
LOG2_E = 1.4426950408889634
FFT_INNER = 64
FFT_PAD = 8

LANES = 128
VMEM_LIMIT = 56 * 1024 * 1024

F32 = jnp.float32
BF16 = jnp.bfloat16

_O_RQ, _O_RK, _O_RV, _O_RG, _O_FU, _O_AQ, _O_AK, _O_CB, _O_CC, _O_CH = (
    0, 256, 512, 768, 1024, 1280, 1536, 1792, 2048, 2304)


def _params(*sem):
    return pltpu.CompilerParams(dimension_semantics=sem, vmem_limit_bytes=VMEM_LIMIT)


def _const_spec(shape):
    nd = len(shape)
    return pl.BlockSpec(shape, lambda *_: (0,) * nd, pipeline_mode=pl.Buffered(1))


def _rms(x, w):
    ms = jnp.mean(x * x, axis=-1, keepdims=True)
    return x * lax.rsqrt(ms + NORM_EPS) * w


def _pair_swap(v, half):
    parts = []
    for s in range(v.shape[-1] // LANES):
        vs = v[:, s * LANES:(s + 1) * LANES]
        lane = lax.broadcasted_iota(jnp.int32, vs.shape, 1)
        lo = (lane & half) == 0
        parts.append(jnp.where(lo, pltpu.roll(vs, LANES - half, 1), pltpu.roll(vs, half, 1)))
    return parts[0] if len(parts) == 1 else jnp.concatenate(parts, axis=-1)


def _dup_heads(v):
    sw = pltpu.roll(v, HEAD_DIM, 1)
    lane = lax.broadcasted_iota(jnp.int32, v.shape, 1)
    lo = lane < HEAD_DIM
    return jnp.concatenate([jnp.where(lo, v, sw), jnp.where(lo, sw, v)], axis=-1)


def _in_kernel(x_ref, nw_ref, w_ref, rcos_ref, rsin_ref, acos_ref, asin_ref, qw_ref, kw_ref, g_ref,
               rq_ref, rkt_ref, rv_ref, rg_ref, fu_ref, aq_ref, akt_ref, av_ref, cb_ref, z_ref):
    h = _rms(x_ref[0], nw_ref[...]).astype(BF16)

    def seg(off, width=MIXER_W):
        return jnp.dot(h, w_ref[:, off:off + width], preferred_element_type=F32)

    aq = seg(_O_AQ)
    akv = seg(_O_AK)

    rcos, rsin = rcos_ref[...], rsin_ref[...]
    q = seg(_O_RQ)
    rq_ref[0] = (q * rcos + _pair_swap(q, HEAD_DIM // 2) * rsin).astype(BF16)
    k = seg(_O_RK)
    k = (k * rcos + _pair_swap(k, HEAD_DIM // 2) * rsin) * (HEAD_DIM ** -0.5)
    rkt_ref[0] = k.T.astype(BF16)

    acos, asin = acos_ref[...], asin_ref[...]
    gm = g_ref[...]
    ms = jnp.dot((aq * aq).astype(BF16), gm, preferred_element_type=F32)
    q = aq * lax.rsqrt(ms + NORM_EPS) * qw_ref[...]
    q = (q * acos + _pair_swap(q, HEAD_DIM // 4) * asin) * (HEAD_DIM ** -0.5 * LOG2_E)
    aq_ref[0] = q.astype(BF16)
    v = akv[:, ATT_KV_W:]
    av_ref[0] = jnp.concatenate([v, jnp.ones_like(v)], axis=-1).astype(BF16)
    k = akv[:, :ATT_KV_W]
    ms = jnp.dot((k * k).astype(BF16), gm[:ATT_KV_W, :ATT_KV_W], preferred_element_type=F32)
    k = k * lax.rsqrt(ms + NORM_EPS) * kw_ref[...]
    k = k * acos[:, :ATT_KV_W] + _pair_swap(k, HEAD_DIM // 4) * asin[:, :ATT_KV_W]
    akt_ref[0] = _dup_heads(k).T.astype(BF16)

    rv_ref[0] = seg(_O_RV).astype(BF16)
    g = seg(_O_RG)
    rg_ref[0] = (g * jax.nn.sigmoid(g)).astype(BF16)
    fu_ref[0] = seg(_O_FU)

    cb_ref[0] = seg(_O_CB).astype(BF16)
    z_ref[0] = (seg(_O_CC) * seg(_O_CH)).astype(BF16)


def _in_proj(x, nw, w_in, tabs, qw, kw, gmean, tm):
    b, s, _ = x.shape
    rcos, rsin, acos, asin = tabs
    tok = pl.BlockSpec((1, tm, MIXER_W), lambda i, j: (i, j, 0))
    tok_t = pl.BlockSpec((1, MIXER_W, tm), lambda i, j: (i, 0, j))
    tab = pl.BlockSpec((tm, MIXER_W), lambda i, j: (j, 0))
    nat = jax.ShapeDtypeStruct((b, s, MIXER_W), BF16)
    tr = jax.ShapeDtypeStruct((b, MIXER_W, s), BF16)
    return pl.pallas_call(
        _in_kernel,
        grid=(b, s // tm),
        in_specs=[pl.BlockSpec((1, tm, D_MODEL), lambda i, j: (i, j, 0)),
                  _const_spec((1, D_MODEL)), _const_spec((D_MODEL, PROJ_W)),
                  tab, tab, tab, tab,
                  _const_spec((1, MIXER_W)), _const_spec((1, ATT_KV_W)), _const_spec((MIXER_W, MIXER_W))],
        out_specs=[tok, tok_t, tok, tok, tok, tok, tok_t, tok, tok, tok],
        out_shape=[nat, tr, nat, nat, jax.ShapeDtypeStruct((b, s, MIXER_W), F32), nat, tr, nat, nat, nat],
        compiler_params=_params("parallel", "parallel"),
    )(x, nw, w_in, rcos, rsin, acos, asin, qw, kw, gmean)


def _ret_kernel(q_ref, kt_ref, v_ref, g_ref, dec_ref, zf_ref, zb_ref, xf_ref, xb_ref, gf_ref, gb_ref,
                bd_ref, gm_ref, gnw_ref, o_ref, sf_ref, sb_ref, curf_ref, curb_ref, p_ref, acc_ref, dev_ref):
    c = RET_CHUNK
    s = q_ref.shape[1]
    nc = s // c
    head = lax.broadcasted_iota(jnp.int32, (c, MIXER_W), 1) // HEAD_DIM

    def rows(n, size=c):
        return pl.ds(pl.multiple_of(n * size, size), size)

    curf_ref[...] = jnp.zeros_like(curf_ref)
    curb_ref[...] = jnp.zeros_like(curb_ref)

    def scan(i, carry):
        for n, cur, hist, zeta, gdec in ((i, curf_ref, sf_ref, zf_ref, gf_ref),
                                         (nc - 1 - i, curb_ref, sb_ref, zb_ref, gb_ref)):
            r = rows(n)
            state = cur[...]
            hist[n] = state.astype(BF16)
            kz = (kt_ref[0, :, r].astype(F32) * zeta[...]).astype(BF16)
            cur[...] = state * gdec[...] + jnp.dot(kz, v_ref[0, r, :], preferred_element_type=F32) * bd_ref[...]
        r = rows(i)
        qc = q_ref[0, r, :]
        qs = jnp.concatenate([jnp.where(head == h, qc, jnp.zeros_like(qc)) for h in range(RET_HEADS)], axis=0)
        p_ref[i] = (jnp.dot(qs, kt_ref[0, :, r], preferred_element_type=F32) * dec_ref[...]).astype(BF16)
        return carry

    lax.fori_loop(0, nc, scan, 0, unroll=8)

    def mix(n, carry):
        r = rows(n)
        qc = q_ref[0, r, :]
        inner = jnp.dot(p_ref[n], v_ref[0, r, :], preferred_element_type=F32)
        o = (jnp.dot(qc, sf_ref[n], preferred_element_type=F32) * xf_ref[...]
             + jnp.dot(qc, sb_ref[n], preferred_element_type=F32) * xb_ref[...])
        for h in range(RET_HEADS):
            o = o + jnp.where(head == h, inner[h * c:(h + 1) * c], 0.0)
        acc_ref[r, :] = o
        return carry

    lax.fori_loop(0, nc, mix, 0, unroll=8)

    nb = 512

    def center(n, carry):
        r = rows(n, nb)
        o = acc_ref[r, :]
        hi = o.astype(BF16)
        lo = (o - hi.astype(F32)).astype(BF16)
        gm = gm_ref[...]
        dev_ref[r, :] = o - (jnp.dot(hi, gm, preferred_element_type=F32) + jnp.dot(lo, gm, preferred_element_type=F32))
        return carry

    lax.fori_loop(0, s // nb, center, 0, unroll=4)

    def scale(n, carry):
        r = rows(n, nb)
        d = dev_ref[r, :]
        var = jnp.dot((d * d).astype(BF16), gm_ref[...], preferred_element_type=F32)
        y = d * lax.rsqrt(var + NORM_EPS) * gnw_ref[...]
        o_ref[0, r, :] = (g_ref[0, r, :].astype(F32) * y).astype(BF16)
        return carry

    lax.fori_loop(0, s // nb, scale, 0, unroll=4)


def _retention(rq, rkt, rv, rg, tabs, gmean, gnw):
    b, s, _ = rq.shape
    nat = pl.BlockSpec((1, s, MIXER_W), lambda i: (i, 0, 0))
    tr = pl.BlockSpec((1, MIXER_W, s), lambda i: (i, 0, 0))
    return pl.pallas_call(
        _ret_kernel,
        grid=(b,),
        in_specs=[nat, tr, nat, nat] + [_const_spec(t.shape) for t in tabs]
                 + [_const_spec((MIXER_W, MIXER_W)), _const_spec((1, MIXER_W))],
        out_specs=nat,
        out_shape=jax.ShapeDtypeStruct((b, s, MIXER_W), BF16),
        scratch_shapes=[pltpu.VMEM((s // RET_CHUNK, MIXER_W, MIXER_W), BF16),
                        pltpu.VMEM((s // RET_CHUNK, MIXER_W, MIXER_W), BF16),
                        pltpu.VMEM((MIXER_W, MIXER_W), F32), pltpu.VMEM((MIXER_W, MIXER_W), F32),
                        pltpu.VMEM((s // RET_CHUNK, RET_HEADS * RET_CHUNK, RET_CHUNK), BF16),
                        pltpu.VMEM((s, MIXER_W), F32), pltpu.VMEM((s, MIXER_W), F32)],
        compiler_params=_params("parallel"),
    )(rq, rkt, rv, rg, *tabs, gmean, gnw)


def _retention_tables(log_rate):
    c = RET_CHUNK
    log_g = -jnp.exp(log_rate.astype(F32))
    lf, lb = log_g[0], log_g[1]
    idx = jnp.arange(c, dtype=F32)
    diff = idx[:, None] - idx[None, :]
    low = diff >= 0
    up = diff < 0
    dec_f = jnp.where(low[None], jnp.exp(jnp.where(low, diff, 0.0)[None] * lf[:, None, None]), 0.0)
    dec_b = jnp.where(up[None], jnp.exp(jnp.where(up, -diff, 0.0)[None] * lb[:, None, None]), 0.0)
    dec = (dec_f + dec_b).reshape(RET_HEADS * c, c)
    lane = lambda per_head: jnp.repeat(per_head, HEAD_DIM, axis=-1)
    zeta_f = lane(jnp.exp((c - 1 - idx)[:, None] * lf[None, :])).T
    zeta_b = lane(jnp.exp(idx[:, None] * lb[None, :])).T
    xi_f = lane(jnp.exp((idx + 1.0)[:, None] * lf[None, :]))
    xi_b = lane(jnp.exp((c - idx)[:, None] * lb[None, :]))
    hd = jnp.arange(MIXER_W) // HEAD_DIM
    bd = (hd[:, None] == hd[None, :]).astype(F32)
    g_f = bd * lane(jnp.exp(c * lf)[None, :])
    g_b = bd * lane(jnp.exp(c * lb)[None, :])
    return dec, zeta_f, zeta_b, xi_f, xi_b, g_f, g_b, bd


def _fft_kernel(x_ref, f1_ref, tc_ref, ts_ref, g2_ref, c_ref, s_ref, o_ref, xp_ref, are_ref, aim_ref, y_ref, *, scale):
    s = x_ref.shape[1]
    n2 = FFT_INNER
    n1 = s // n2
    p1, p2 = n1 + FFT_PAD, n2 + FFT_PAD
    halves = (slice(0, LANES), slice(LANES, 2 * LANES))

    def put(ref, rows, val):
        for h, lanes in enumerate(halves):
            ref[h, rows, :] = val[:, lanes]

    def get(ref, rows):
        return jnp.concatenate([ref[0, rows, :], ref[1, rows, :]], axis=-1)

    for t1 in range(n1):
        put(xp_ref, slice(t1 * p2, t1 * p2 + n2), x_ref[0, t1 * n2:(t1 + 1) * n2, :])
    f1 = f1_ref[...]
    for t2 in range(n2):
        xs = get(xp_ref, pl.ds(t2, n1, stride=p2)).astype(BF16)
        a = jnp.dot(f1, xs, preferred_element_type=F32)
        ar, ai = a[:n1], a[n1:]
        rows = slice(t2 * n1, (t2 + 1) * n1)
        tc = jnp.concatenate([tc_ref[rows, :]] * 2, axis=-1)
        ts = jnp.concatenate([ts_ref[rows, :]] * 2, axis=-1)
        rows = slice(t2 * p1, t2 * p1 + n1)
        put(are_ref, rows, ar * tc + ai * ts)
        put(aim_ref, rows, ai * tc - ar * ts)
    group = 4
    for f0 in range(0, n1, group):
        zr, zi = [], []
        for f in range(f0, f0 + group):
            rows = pl.ds(f, n2, stride=p1)
            r = jnp.concatenate([get(are_ref, rows), get(aim_ref, rows)], axis=0)
            z = jnp.dot(g2_ref[...], r.astype(BF16), preferred_element_type=F32)
            zr.append(z[:n2])
            zi.append(z[n2:])
        y = (jnp.dot(jnp.concatenate(zr, axis=0).astype(BF16), c_ref[...], preferred_element_type=F32)
             + jnp.dot(jnp.concatenate(zi, axis=0).astype(BF16), s_ref[...], preferred_element_type=F32)) * scale
        for i, f in enumerate(range(f0, f0 + group)):
            put(y_ref, slice(f * p2, f * p2 + n2), y[i * n2:(i + 1) * n2])
    for f2 in range(n2):
        o_ref[0, f2 * n1:(f2 + 1) * n1, :] = get(y_ref, pl.ds(f2, n1, stride=p2)).astype(BF16)


def _dft_cs(n):
    j = np.arange(n)
    ang = 2.0 * np.pi * ((j[:, None] * j[None, :]) % n) / n
    return np.cos(ang), np.sin(ang)


def _fourier(fu):
    b, s, w = fu.shape
    n2 = FFT_INNER
    n1 = s // n2
    c1, s1 = _dft_cs(n1)
    f1 = jnp.asarray(np.concatenate([c1, -s1], axis=0), F32).astype(BF16)
    ang = 2.0 * np.pi * (np.arange(n2)[:, None] * np.arange(n1)[None, :]) / s
    tw_c = jnp.broadcast_to(jnp.asarray(np.cos(ang).reshape(s, 1), F32), (s, LANES))
    tw_s = jnp.broadcast_to(jnp.asarray(np.sin(ang).reshape(s, 1), F32), (s, LANES))
    c2, s2 = _dft_cs(n2)
    g2 = jnp.asarray(np.block([[c2, s2], [-s2, c2]]), F32).astype(BF16)
    cc, sc = _dft_cs(HEAD_DIM)
    eye = np.eye(w // HEAD_DIM)
    cbd = jnp.asarray(np.kron(eye, cc), F32).astype(BF16)
    sbd = jnp.asarray(np.kron(eye, sc), F32).astype(BF16)

    seq = pl.BlockSpec((1, s, w), lambda i: (i, 0, 0))
    return pl.pallas_call(
        functools.partial(_fft_kernel, scale=float((s * HEAD_DIM) ** -0.5)),
        grid=(b,),
        in_specs=[pl.BlockSpec((1, s, w), lambda i: (i, 0, 0)),
                  _const_spec((2 * n1, n1)), _const_spec((s, LANES)), _const_spec((s, LANES)),
                  _const_spec((2 * n2, 2 * n2)), _const_spec((w, w)), _const_spec((w, w))],
        out_specs=seq,
        out_shape=jax.ShapeDtypeStruct((b, s, w), BF16),
        scratch_shapes=[pltpu.VMEM((w // LANES, n1 * (n2 + FFT_PAD), LANES), F32),
                        pltpu.VMEM((w // LANES, n2 * (n1 + FFT_PAD), LANES), F32),
                        pltpu.VMEM((w // LANES, n2 * (n1 + FFT_PAD), LANES), F32),
                        pltpu.VMEM((w // LANES, n1 * (n2 + FFT_PAD), LANES), F32)],
        compiler_params=_params("parallel"),
    )(fu, f1, tw_c, tw_s, g2, cbd, sbd)


def _att_kernel(q_ref, kt_ref, v_ref, o_ref, *, kv_block):
    q = q_ref[0]
    tq = q.shape[0]
    s = kt_ref.shape[2]
    head = lax.broadcasted_iota(jnp.int32, q.shape, 1) // HEAD_DIM
    qs = jnp.concatenate([jnp.where(head == h, q, jnp.zeros_like(q)) for h in range(ATT_Q_HEADS)], axis=0)
    m = acc = None
    for j in range(s // kv_block):
        cols = slice(j * kv_block, (j + 1) * kv_block)
        sc = jnp.dot(qs, kt_ref[0, :, cols], preferred_element_type=F32)
        bm = jnp.max(sc, axis=-1, keepdims=True)
        m_new = bm if m is None else jnp.maximum(m, bm)
        pv = jnp.dot(jnp.exp2(sc - m_new).astype(BF16), v_ref[0, cols, :], preferred_element_type=F32)
        acc = pv if m is None else acc * jnp.exp2(m - m_new) + pv
        m = m_new
    o = [acc[h * tq:(h + 1) * tq, :LANES] / acc[h * tq:(h + 1) * tq, LANES:] for h in range(ATT_Q_HEADS)]
    lo = lax.broadcasted_iota(jnp.int32, (tq, LANES), 1) < HEAD_DIM
    out = jnp.concatenate([jnp.where(lo, o[0], pltpu.roll(o[1], HEAD_DIM, 1)),
                           jnp.where(lo, pltpu.roll(o[2], HEAD_DIM, 1), o[3])], axis=-1)
    o_ref[0] = out.astype(BF16)


def _attention(aq, akt, av, tq, kv_block=512):
    b, s, w = aq.shape
    return pl.pallas_call(
        functools.partial(_att_kernel, kv_block=kv_block),
        grid=(b, s // tq),
        in_specs=[pl.BlockSpec((1, tq, w), lambda i, j: (i, j, 0)),
                  pl.BlockSpec((1, w, s), lambda i, j: (i, 0, 0)),
                  pl.BlockSpec((1, s, w), lambda i, j: (i, 0, 0))],
        out_specs=pl.BlockSpec((1, tq, w), lambda i, j: (i, j, 0)),
        out_shape=jax.ShapeDtypeStruct((b, s, w), BF16),
        compiler_params=_params("parallel", "parallel"),
    )(aq, akt, av)


def _post_kernel(x_ref, ret_ref, fft_ref, att_ref, cb_ref, z_ref, halo_ref, cw_ref, wo_ref,
                 nf_ref, w1_ref, w2_ref, np_ref, wg_ref, p_ref, wp_ref, fn_ref, o_ref, *, final, ff_block):
    tm = x_ref.shape[1]
    z = z_ref[0].astype(F32)
    row = lax.broadcasted_iota(jnp.int32, z.shape, 0)
    halo = halo_ref[0, 0].astype(F32)
    z_prev = jnp.where(row == 0, halo[0:1], pltpu.roll(z, 1, 0))
    z_next = jnp.where(row == tm - 1, halo[1:2], pltpu.roll(z, tm - 1, 0))
    cw = cw_ref[...]
    conv = cb_ref[0].astype(F32) * (z_prev * cw[0:1] + z * cw[1:2] + z_next * cw[2:3])
    mixed = jnp.concatenate([ret_ref[0], fft_ref[0], att_ref[0], conv.astype(BF16)], axis=-1)
    x = x_ref[0] + jnp.dot(mixed, wo_ref[...], preferred_element_type=F32)

    hn = _rms(x, nf_ref[...]).astype(BF16)
    acc = x
    for j in range(D_FF // ff_block):
        hf = jnp.dot(hn, w1_ref[:, j * ff_block:(j + 1) * ff_block], preferred_element_type=F32)
        act = jnp.square(jnp.maximum(hf, 0.0)).astype(BF16)
        acc = acc + jnp.dot(act, w2_ref[j * ff_block:(j + 1) * ff_block, :], preferred_element_type=F32)
    x = acc

    gate = jax.nn.sigmoid(jnp.dot(_rms(x, np_ref[...]).astype(BF16), wg_ref[...], preferred_element_type=F32))
    x = x + gate * jnp.dot(p_ref[0, 0].astype(BF16), wp_ref[...], preferred_element_type=F32)
    if final:
        x = _rms(x, fn_ref[...])
    o_ref[0] = x


def _post(x, y_ret, y_fft, y_att, cb, z, conv_w, w_out, nf, w1, w2, npl, wg, p, layer, wp, fn, tm, final):
    b, s, _ = x.shape
    nt = s // tm
    zero = jnp.zeros((b, 1, MIXER_W), z.dtype)
    prev = jnp.concatenate([zero, z[:, tm - 1::tm][:, :-1]], axis=1)
    nxt = jnp.concatenate([z[:, ::tm][:, 1:], zero], axis=1)
    halo = jnp.concatenate([prev[:, :, None], nxt[:, :, None],
                            jnp.zeros((b, nt, 6, MIXER_W), z.dtype)], axis=2)
    tok = pl.BlockSpec((1, tm, MIXER_W), lambda i, j: (i, j, 0))
    xs = pl.BlockSpec((1, tm, D_MODEL), lambda i, j: (i, j, 0))
    return pl.pallas_call(
        functools.partial(_post_kernel, final=final, ff_block=1024),
        grid=(b, nt),
        in_specs=[xs, tok, tok, tok, tok, tok,
                  pl.BlockSpec((1, 1, 8, MIXER_W), lambda i, j: (i, j, 0, 0)),
                  _const_spec((3, MIXER_W)), _const_spec((D_MODEL, D_MODEL)),
                  _const_spec((1, D_MODEL)), _const_spec((D_MODEL, D_FF)), _const_spec((D_FF, D_MODEL)),
                  _const_spec((1, D_MODEL)), _const_spec((D_MODEL, D_MODEL)),
                  pl.BlockSpec((1, 1, tm, PLE_DIM), lambda i, j: (layer, i, j, 0)),
                  _const_spec((PLE_DIM, D_MODEL)), _const_spec((1, D_MODEL))],
        out_specs=xs,
        out_shape=jax.ShapeDtypeStruct((b, s, D_MODEL), F32),
        compiler_params=_params("parallel", "parallel"),
    )(x, y_ret, y_fft, y_att, cb, z, halo, conv_w, w_out, nf, w1, w2, npl, wg, p, wp, fn)


def _rope_tables(s):
    t = jnp.arange(s, dtype=F32)
    half = HEAD_DIM // 2
    inv = ROPE_BASE ** (-jnp.arange(half, dtype=F32) / half)
    ang = t[:, None] * inv[None, :]
    cos, sin = jnp.cos(ang), jnp.sin(ang)
    rcos = jnp.tile(jnp.concatenate([cos, cos], axis=-1), (1, RET_HEADS))
    rsin = jnp.tile(jnp.concatenate([-sin, sin], axis=-1), (1, RET_HEADS))
    quarter = half // 2
    inv_a = ROPE_BASE ** (-jnp.arange(quarter, dtype=F32) / quarter)
    ang_r = (jnp.arange(s) // GRID_W).astype(F32)[:, None] * inv_a[None, :]
    ang_c = (jnp.arange(s) % GRID_W).astype(F32)[:, None] * inv_a[None, :]
    cr, sr, cc, sc = jnp.cos(ang_r), jnp.sin(ang_r), jnp.cos(ang_c), jnp.sin(ang_c)
    acos = jnp.tile(jnp.concatenate([cr, cr, cc, cc], axis=-1), (1, ATT_Q_HEADS))
    asin = jnp.tile(jnp.concatenate([-sr, sr, -sc, sc], axis=-1), (1, ATT_Q_HEADS))
    return rcos, rsin, acos, asin


def _encode(x, p, wts, depth):
    b, s, _ = x.shape
    tabs = _rope_tables(s)
    hd = np.arange(MIXER_W) // HEAD_DIM
    gmean = jnp.asarray((hd[:, None] == hd[None, :]) / HEAD_DIM, F32).astype(BF16)
    for l in range(depth):
        w = wts[l]
        rq, rkt, rv, rg, fu, aq, akt, av, cb, z = _in_proj(
            x, w["norm_mix"], w["w_in"], tabs, w["q_norm"], w["k_norm"], gmean, tm=1024)
        y_ret = _retention(rq, rkt, rv, rg, w["ret_tabs"], gmean, w["ret_gn"])
        y_fft = _fourier(fu)
        y_att = _attention(aq, akt, av, tq=1024, kv_block=256)
        x = _post(x, y_ret, y_fft, y_att, cb, z, w["conv_w"], w["w_out"], w["norm_ffn"], w["w_ffn1"],
                  w["w_ffn2"], w["norm_pl"], w["w_pl_gate"], p, l, w["w_pl_proj"], w["final_norm"],
                  tm=512, final=(l == depth - 1))
    return x


def kernel(x_prompt, x_sample, p_prompt, p_sample, norm_mix_w, w_in, ret_log_rate, ret_gn_w, q_norm_w, k_norm_w,
           conv_w, w_out, norm_ffn_w, w_ffn1, w_ffn2, norm_pl_w, w_pl_gate, w_pl_proj, final_norm_w):
    depth = w_in.shape[0]
    wts = []
    for l in range(depth):
        wts.append(dict(
            norm_mix=norm_mix_w[l][None, :], w_in=w_in[l].astype(BF16),
            ret_tabs=_retention_tables(ret_log_rate[l]), ret_gn=ret_gn_w[l][None, :],
            q_norm=jnp.tile(q_norm_w[l], ATT_Q_HEADS)[None, :], k_norm=jnp.tile(k_norm_w[l], ATT_KV_HEADS)[None, :],
            conv_w=conv_w[l], w_out=w_out[l].astype(BF16), norm_ffn=norm_ffn_w[l][None, :],
            w_ffn1=w_ffn1[l].astype(BF16), w_ffn2=w_ffn2[l].astype(BF16), norm_pl=norm_pl_w[l][None, :],
            w_pl_gate=w_pl_gate[l].astype(BF16), w_pl_proj=w_pl_proj[l].astype(BF16),
            final_norm=final_norm_w[None, :]))
    return _encode(x_prompt, p_prompt, wts, depth), _encode(x_sample, p_sample, wts, depth)
```

```python
import functools

import numpy as np
import jax
import jax.numpy as jnp
from jax import lax
from jax.experimental import pallas as pl
from jax.experimental.pallas import tpu as pltpu

D_MODEL = 1024
HEAD_DIM = 64
RET_HEADS = 4
ATT_Q_HEADS = 4
ATT_KV_HEADS = 2
MIXER_W = 256
ATT_KV_W = ATT_KV_HEADS * HEAD_DIM
PROJ_W = 2560
D_FF = 4 * D_MODEL
PLE_DIM = 256
GRID_W = 64
RET_CHUNK = 256
RET_NORM_ROWS = 512
ROPE_BASE = 10000.0
NORM_EPS = 1e-6
LOG2_E = 1.4426950408889634
FFT_INNER = 64
FFT_PAD = 8

LANES = 128
VMEM_LIMIT = 56 * 1024 * 1024

F32 = jnp.float32
BF16 = jnp.bfloat16

_O_RQ, _O_RK, _O_RV, _O_RG, _O_FU, _O_AQ, _O_AK, _O_CB, _O_CC, _O_CH = (
    0, 256, 512, 768, 1024, 1280, 1536, 1792, 2048, 2304)


def _params(*sem):
    return pltpu.CompilerParams(dimension_semantics=sem, vmem_limit_bytes=VMEM_LIMIT)


def _const_spec(shape):
    nd = len(shape)
    return pl.BlockSpec(shape, lambda *_: (0,) * nd, pipeline_mode=pl.Buffered(1))


def _rms(x, w):
    ms = jnp.mean(x * x, axis=-1, keepdims=True)
    return x * lax.rsqrt(ms + NORM_EPS) * w


def _pair_swap(v, half):
    parts = []
    for s in range(v.shape[-1] // LANES):
        vs = v[:, s * LANES:(s + 1) * LANES]
        lane = lax.broadcasted_iota(jnp.int32, vs.shape, 1)
        lo = (lane & half) == 0
        parts.append(jnp.where(lo, pltpu.roll(vs, LANES - half, 1), pltpu.roll(vs, half, 1)))
    return parts[0] if len(parts) == 1 else jnp.concatenate(parts, axis=-1)


def _dup_heads(v):
    sw = pltpu.roll(v, HEAD_DIM, 1)
    lane = lax.broadcasted_iota(jnp.int32, v.shape, 1)
    lo = lane < HEAD_DIM
    return jnp.concatenate([jnp.where(lo, v, sw), jnp.where(lo, sw, v)], axis=-1)


def _in_kernel(x_ref, nw_ref, w_ref, rcos_ref, rsin_ref, acos_ref, asin_ref, qw_ref, kw_ref, g_ref,
               rq_ref, rkt_ref, rv_ref, rg_ref, fu_ref, aq_ref, akt_ref, av_ref, cb_ref, z_ref):
    h = _rms(x_ref[0], nw_ref[...]).astype(BF16)

    def seg(off, width=MIXER_W):
        return jnp.dot(h, w_ref[:, off:off + width], preferred_element_type=F32)

    aq = seg(_O_AQ)
    akv = seg(_O_AK)

    rcos, rsin = rcos_ref[...], rsin_ref[...]
    q = seg(_O_RQ)
    rq_ref[0] = (q * rcos + _pair_swap(q, HEAD_DIM // 2) * rsin).astype(BF16)
    k = seg(_O_RK)
    k = (k * rcos + _pair_swap(k, HEAD_DIM // 2) * rsin) * (HEAD_DIM ** -0.5)
    rkt_ref[0] = k.T.astype(BF16)

    acos, asin = acos_ref[...], asin_ref[...]
    gm = g_ref[...]
    ms = jnp.dot((aq * aq).astype(BF16), gm, preferred_element_type=F32)
    q = aq * lax.rsqrt(ms + NORM_EPS) * qw_ref[...]
    q = (q * acos + _pair_swap(q, HEAD_DIM // 4) * asin) * (HEAD_DIM ** -0.5 * LOG2_E)
    aq_ref[0] = q.astype(BF16)
    v = akv[:, ATT_KV_W:]
    av_ref[0] = jnp.concatenate([v, jnp.ones_like(v)], axis=-1).astype(BF16)
    k = akv[:, :ATT_KV_W]
    ms = jnp.dot((k * k).astype(BF16), gm[:ATT_KV_W, :ATT_KV_W], preferred_element_type=F32)
    k = k * lax.rsqrt(ms + NORM_EPS) * kw_ref[...]
    k = k * acos[:, :ATT_KV_W] + _pair_swap(k, HEAD_DIM // 4) * asin[:, :ATT_KV_W]
    akt_ref[0] = _dup_heads(k).T.astype(BF16)

    rv_ref[0] = seg(_O_RV).astype(BF16)
    g = seg(_O_RG)
    rg_ref[0] = (g * jax.nn.sigmoid(g)).astype(BF16)
    fu_ref[0] = seg(_O_FU)

    cb_ref[0] = seg(_O_CB).astype(BF16)
    z_ref[0] = (seg(_O_CC) * seg(_O_CH)).astype(BF16)


def _in_proj(x, nw, w_in, tabs, qw, kw, gmean, tm):
    b, s, _ = x.shape
    rcos, rsin, acos, asin = tabs
    tok = pl.BlockSpec((1, tm, MIXER_W), lambda i, j: (i, j, 0))
    tok_t = pl.BlockSpec((1, MIXER_W, tm), lambda i, j: (i, 0, j))
    tab = pl.BlockSpec((tm, MIXER_W), lambda i, j: (j, 0))
    nat = jax.ShapeDtypeStruct((b, s, MIXER_W), BF16)
    tr = jax.ShapeDtypeStruct((b, MIXER_W, s), BF16)
    return pl.pallas_call(
        _in_kernel,
        grid=(b, s // tm),
        in_specs=[pl.BlockSpec((1, tm, D_MODEL), lambda i, j: (i, j, 0)),
                  _const_spec((1, D_MODEL)), _const_spec((D_MODEL, PROJ_W)),
                  tab, tab, tab, tab,
                  _const_spec((1, MIXER_W)), _const_spec((1, ATT_KV_W)), _const_spec((MIXER_W, MIXER_W))],
        out_specs=[tok, tok_t, tok, tok, tok, tok, tok_t, tok, tok, tok],
        out_shape=[nat, tr, nat, nat, jax.ShapeDtypeStruct((b, s, MIXER_W), F32), nat, tr, nat, nat, nat],
        compiler_params=_params("parallel", "parallel"),
    )(x, nw, w_in, rcos, rsin, acos, asin, qw, kw, gmean)


def _ret_kernel(q_ref, kt_ref, v_ref, g_ref, dec_ref, zf_ref, zb_ref, xf_ref, xb_ref, gf_ref, gb_ref,
                bd_ref, gm_ref, gnw_ref, o_ref, sf_ref, sb_ref, curf_ref, curb_ref, p_ref, acc_ref, dev_ref):
    c = RET_CHUNK
    s = q_ref.shape[1]
    nc = s // c
    head = lax.broadcasted_iota(jnp.int32, (c, MIXER_W), 1) // HEAD_DIM

    def rows(n, size=c):
        return pl.ds(pl.multiple_of(n * size, size), size)

    curf_ref[...] = jnp.zeros_like(curf_ref)
    curb_ref[...] = jnp.zeros_like(curb_ref)

    def scan(i, carry):
        for n, cur, hist, zeta, gdec in ((i, curf_ref, sf_ref, zf_ref, gf_ref),
                                         (nc - 1 - i, curb_ref, sb_ref, zb_ref, gb_ref)):
            r = rows(n)
            state = cur[...]
            hist[n] = state.astype(BF16)
            kz = (kt_ref[0, :, r].astype(F32) * zeta[...]).astype(BF16)
            cur[...] = state * gdec[...] + jnp.dot(kz, v_ref[0, r, :], preferred_element_type=F32) * bd_ref[...]
        r = rows(i)
        qc = q_ref[0, r, :]
        qs = jnp.concatenate([jnp.where(head == h, qc, jnp.zeros_like(qc)) for h in range(RET_HEADS)], axis=0)
        p_ref[i] = (jnp.dot(qs, kt_ref[0, :, r], preferred_element_type=F32) * dec_ref[...]).astype(BF16)
        return carry

    lax.fori_loop(0, nc, scan, 0, unroll=8)

    def mix(n, carry):
        r = rows(n)
        qc = q_ref[0, r, :]
        inner = jnp.dot(p_ref[n], v_ref[0, r, :], preferred_element_type=F32)
        o = (jnp.dot(qc, sf_ref[n], preferred_element_type=F32) * xf_ref[...]
             + jnp.dot(qc, sb_ref[n], preferred_element_type=F32) * xb_ref[...])
        for h in range(RET_HEADS):
            o = o + jnp.where(head == h, inner[h * c:(h + 1) * c], 0.0)
        acc_ref[r, :] = o
        return carry

    lax.fori_loop(0, nc, mix, 0, unroll=8)

    nb = RET_NORM_ROWS

    def center(n, carry):
        r = rows(n, nb)
        o = acc_ref[r, :]
        hi = o.astype(BF16)
        lo = (o - hi.astype(F32)).astype(BF16)
        gm = gm_ref[...]
        dev_ref[r, :] = o - (jnp.dot(hi, gm, preferred_element_type=F32) + jnp.dot(lo, gm, preferred_element_type=F32))
        return carry

    lax.fori_loop(0, s // nb, center, 0, unroll=4)

    def scale(n, carry):
        r = rows(n, nb)
        d = dev_ref[r, :]
        var = jnp.dot((d * d).astype(BF16), gm_ref[...], preferred_element_type=F32)
        y = d * lax.rsqrt(var + NORM_EPS) * gnw_ref[...]
        o_ref[0, r, :] = (g_ref[0, r, :].astype(F32) * y).astype(BF16)
        return carry

    lax.fori_loop(0, s // nb, scale, 0, unroll=4)


def _retention(rq, rkt, rv, rg, tabs, gmean, gnw):
    b, s, _ = rq.shape
    nat = pl.BlockSpec((1, s, MIXER_W), lambda i: (i, 0, 0))
    tr = pl.BlockSpec((1, MIXER_W, s), lambda i: (i, 0, 0))
    return pl.pallas_call(
        _ret_kernel,
        grid=(b,),
        in_specs=[nat, tr, nat, nat] + [_const_spec(t.shape) for t in tabs]
                 + [_const_spec((MIXER_W, MIXER_W)), _const_spec((1, MIXER_W))],
        out_specs=nat,
        out_shape=jax.ShapeDtypeStruct((b, s, MIXER_W), BF16),
        scratch_shapes=[pltpu.VMEM((s // RET_CHUNK, MIXER_W, MIXER_W), BF16),
                        pltpu.VMEM((s // RET_CHUNK, MIXER_W, MIXER_W), BF16),
                        pltpu.VMEM((MIXER_W, MIXER_W), F32), pltpu.VMEM((MIXER_W, MIXER_W), F32),
                        pltpu.VMEM((s // RET_CHUNK, RET_HEADS * RET_CHUNK, RET_CHUNK), BF16),
                        pltpu.VMEM((s, MIXER_W), F32), pltpu.VMEM((s, MIXER_W), F32)],
        compiler_params=_params("parallel"),
    )(rq, rkt, rv, rg, *tabs, gmean, gnw)


def _retention_tables(log_rate):
    c = RET_CHUNK
    log_g = -jnp.exp(log_rate.astype(F32))
    lf, lb = log_g[0], log_g[1]
    idx = jnp.arange(c, dtype=F32)
    diff = idx[:, None] - idx[None, :]
    low = diff >= 0
    up = diff < 0
    dec_f = jnp.where(low[None], jnp.exp(jnp.where(low, diff, 0.0)[None] * lf[:, None, None]), 0.0)
    dec_b = jnp.where(up[None], jnp.exp(jnp.where(up, -diff, 0.0)[None] * lb[:, None, None]), 0.0)
    dec = (dec_f + dec_b).reshape(RET_HEADS * c, c)
    lane = lambda per_head: jnp.repeat(per_head, HEAD_DIM, axis=-1)
    zeta_f = lane(jnp.exp((c - 1 - idx)[:, None] * lf[None, :])).T
    zeta_b = lane(jnp.exp(idx[:, None] * lb[None, :])).T
    xi_f = lane(jnp.exp((idx + 1.0)[:, None] * lf[None, :]))
    xi_b = lane(jnp.exp((c - idx)[:, None] * lb[None, :]))
    hd = jnp.arange(MIXER_W) // HEAD_DIM
    bd = (hd[:, None] == hd[None, :]).astype(F32)
    g_f = bd * lane(jnp.exp(c * lf)[None, :])
    g_b = bd * lane(jnp.exp(c * lb)[None, :])
    return dec, zeta_f, zeta_b, xi_f, xi_b, g_f, g_b, bd


def _fft_kernel(x_ref, f1_ref, tc_ref, ts_ref, g2_ref, c_ref, s_ref, o_ref, xp_ref, are_ref, aim_ref, y_ref, *, scale):
    s = x_ref.shape[1]
    n2 = FFT_INNER
    n1 = s // n2
    p1, p2 = n1 + FFT_PAD, n2 + FFT_PAD
    halves = (slice(0, LANES), slice(LANES, 2 * LANES))

    def put(ref, rows, val):
        for h, lanes in enumerate(halves):
            ref[h, rows, :] = val[:, lanes]

    def get(ref, rows):
        return jnp.concatenate([ref[0, rows, :], ref[1, rows, :]], axis=-1)

    for t1 in range(n1):
        put(xp_ref, slice(t1 * p2, t1 * p2 + n2), x_ref[0, t1 * n2:(t1 + 1) * n2, :])
    f1 = f1_ref[...]
    for t2 in range(n2):
        xs = get(xp_ref, pl.ds(t2, n1, stride=p2)).astype(BF16)
        a = jnp.dot(f1, xs, preferred_element_type=F32)
        ar, ai = a[:n1], a[n1:]
        rows = slice(t2 * n1, (t2 + 1) * n1)
        tc = jnp.concatenate([tc_ref[rows, :]] * 2, axis=-1)
        ts = jnp.concatenate([ts_ref[rows, :]] * 2, axis=-1)
        rows = slice(t2 * p1, t2 * p1 + n1)
        put(are_ref, rows, ar * tc + ai * ts)
        put(aim_ref, rows, ai * tc - ar * ts)
    group = 4
    for f0 in range(0, n1, group):
        zr, zi = [], []
        for f in range(f0, f0 + group):
            rows = pl.ds(f, n2, stride=p1)
            r = jnp.concatenate([get(are_ref, rows), get(aim_ref, rows)], axis=0)
            z = jnp.dot(g2_ref[...], r.astype(BF16), preferred_element_type=F32)
            zr.append(z[:n2])
            zi.append(z[n2:])
        y = (jnp.dot(jnp.concatenate(zr, axis=0).astype(BF16), c_ref[...], preferred_element_type=F32)
             + jnp.dot(jnp.concatenate(zi, axis=0).astype(BF16), s_ref[...], preferred_element_type=F32)) * scale
        for i, f in enumerate(range(f0, f0 + group)):
            put(y_ref, slice(f * p2, f * p2 + n2), y[i * n2:(i + 1) * n2])
    for f2 in range(n2):
        o_ref[0, f2 * n1:(f2 + 1) * n1, :] = get(y_ref, pl.ds(f2, n1, stride=p2)).astype(BF16)


def _dft_cs(n):
    j = np.arange(n)
    ang = 2.0 * np.pi * ((j[:, None] * j[None, :]) % n) / n
    return np.cos(ang), np.sin(ang)


def _fourier(fu):
    b, s, w = fu.shape
    n2 = FFT_INNER
    n1 = s // n2
    c1, s1 = _dft_cs(n1)
    f1 = jnp.asarray(np.concatenate([c1, -s1], axis=0), F32).astype(BF16)
    ang = 2.0 * np.pi * (np.arange(n2)[:, None] * np.arange(n1)[None, :]) / s
    tw_c = jnp.broadcast_to(jnp.asarray(np.cos(ang).reshape(s, 1), F32), (s, LANES))
    tw_s = jnp.broadcast_to(jnp.asarray(np.sin(ang).reshape(s, 1), F32), (s, LANES))
    c2, s2 = _dft_cs(n2)
    g2 = jnp.asarray(np.block([[c2, s2], [-s2, c2]]), F32).astype(BF16)
    cc, sc = _dft_cs(HEAD_DIM)
    eye = np.eye(w // HEAD_DIM)
    cbd = jnp.asarray(np.kron(eye, cc), F32).astype(BF16)
    sbd = jnp.asarray(np.kron(eye, sc), F32).astype(BF16)

    seq = pl.BlockSpec((1, s, w), lambda i: (i, 0, 0))
    return pl.pallas_call(
        functools.partial(_fft_kernel, scale=float((s * HEAD_DIM) ** -0.5)),
        grid=(b,),
        in_specs=[pl.BlockSpec((1, s, w), lambda i: (i, 0, 0)),
                  _const_spec((2 * n1, n1)), _const_spec((s, LANES)), _const_spec((s, LANES)),
                  _const_spec((2 * n2, 2 * n2)), _const_spec((w, w)), _const_spec((w, w))],
        out_specs=seq,
        out_shape=jax.ShapeDtypeStruct((b, s, w), BF16),
        scratch_shapes=[pltpu.VMEM((w // LANES, n1 * (n2 + FFT_PAD), LANES), F32),
                        pltpu.VMEM((w // LANES, n2 * (n1 + FFT_PAD), LANES), F32),
                        pltpu.VMEM((w // LANES, n2 * (n1 + FFT_PAD), LANES), F32),
                        pltpu.VMEM((w // LANES, n1 * (n2 + FFT_PAD), LANES), F32)],
        compiler_params=_params("parallel"),
    )(fu, f1, tw_c, tw_s, g2, cbd, sbd)


def _att_kernel(q_ref, kt_ref, v_ref, o_ref, *, kv_block):
    q = q_ref[0]
    tq = q.shape[0]
    s = kt_ref.shape[2]
    head = lax.broadcasted_iota(jnp.int32, q.shape, 1) // HEAD_DIM
    qs = jnp.concatenate([jnp.where(head == h, q, jnp.zeros_like(q)) for h in range(ATT_Q_HEADS)], axis=0)
    m = acc = None
    for j in range(s // kv_block):
        cols = slice(j * kv_block, (j + 1) * kv_block)
        sc = jnp.dot(qs, kt_ref[0, :, cols], preferred_element_type=F32)
        bm = jnp.max(sc, axis=-1, keepdims=True)
        m_new = bm if m is None else jnp.maximum(m, bm)
        pv = jnp.dot(jnp.exp2(sc - m_new).astype(BF16), v_ref[0, cols, :], preferred_element_type=F32)
        acc = pv if m is None else acc * jnp.exp2(m - m_new) + pv
        m = m_new
    o = [acc[h * tq:(h + 1) * tq, :LANES] / acc[h * tq:(h + 1) * tq, LANES:] for h in range(ATT_Q_HEADS)]
    lo = lax.broadcasted_iota(jnp.int32, (tq, LANES), 1) < HEAD_DIM
    out = jnp.concatenate([jnp.where(lo, o[0], pltpu.roll(o[1], HEAD_DIM, 1)),
                           jnp.where(lo, pltpu.roll(o[2], HEAD_DIM, 1), o[3])], axis=-1)
    o_ref[0] = out.astype(BF16)


def _attention(aq, akt, av, tq, kv_block=512):
    b, s, w = aq.shape
    return pl.pallas_call(
        functools.partial(_att_kernel, kv_block=kv_block),
        grid=(b, s // tq),
        in_specs=[pl.BlockSpec((1, tq, w), lambda i, j: (i, j, 0)),
                  pl.BlockSpec((1, w, s), lambda i, j: (i, 0, 0)),
                  pl.BlockSpec((1, s, w), lambda i, j: (i, 0, 0))],
        out_specs=pl.BlockSpec((1, tq, w), lambda i, j: (i, j, 0)),
        out_shape=jax.ShapeDtypeStruct((b, s, w), BF16),
        compiler_params=_params("parallel", "parallel"),
    )(aq, akt, av)


def _post_kernel(x_ref, ret_ref, fft_ref, att_ref, cb_ref, z_ref, halo_ref, cw_ref, wo_ref,
                 nf_ref, w1_ref, w2_ref, np_ref, wg_ref, p_ref, wp_ref, fn_ref, o_ref, *, final, ff_block):
    tm = x_ref.shape[1]
    z = z_ref[0].astype(F32)
    row = lax.broadcasted_iota(jnp.int32, z.shape, 0)
    halo = halo_ref[0, 0].astype(F32)
    z_prev = jnp.where(row == 0, halo[0:1], pltpu.roll(z, 1, 0))
    z_next = jnp.where(row == tm - 1, halo[1:2], pltpu.roll(z, tm - 1, 0))
    cw = cw_ref[...]
    conv = cb_ref[0].astype(F32) * (z_prev * cw[0:1] + z * cw[1:2] + z_next * cw[2:3])
    mixed = jnp.concatenate([ret_ref[0], fft_ref[0], att_ref[0], conv.astype(BF16)], axis=-1)
    x = x_ref[0] + jnp.dot(mixed, wo_ref[...], preferred_element_type=F32)

    hn = _rms(x, nf_ref[...]).astype(BF16)
    acc = x
    for j in range(D_FF // ff_block):
        hf = jnp.dot(hn, w1_ref[:, j * ff_block:(j + 1) * ff_block], preferred_element_type=F32)
        act = jnp.square(jnp.maximum(hf, 0.0)).astype(BF16)
        acc = acc + jnp.dot(act, w2_ref[j * ff_block:(j + 1) * ff_block, :], preferred_element_type=F32)
    x = acc

    gate = jax.nn.sigmoid(jnp.dot(_rms(x, np_ref[...]).astype(BF16), wg_ref[...], preferred_element_type=F32))
    x = x + gate * jnp.dot(p_ref[0, 0].astype(BF16), wp_ref[...], preferred_element_type=F32)
    if final:
        x = _rms(x, fn_ref[...])
    o_ref[0] = x


def _post(x, y_ret, y_fft, y_att, cb, z, conv_w, w_out, nf, w1, w2, npl, wg, p, layer, wp, fn, tm, final):
    b, s, _ = x.shape
    nt = s // tm
    zero = jnp.zeros((b, 1, MIXER_W), z.dtype)
    prev = jnp.concatenate([zero, z[:, tm - 1::tm][:, :-1]], axis=1)
    nxt = jnp.concatenate([z[:, ::tm][:, 1:], zero], axis=1)
    halo = jnp.concatenate([prev[:, :, None], nxt[:, :, None],
                            jnp.zeros((b, nt, 6, MIXER_W), z.dtype)], axis=2)
    tok = pl.BlockSpec((1, tm, MIXER_W), lambda i, j: (i, j, 0))
    xs = pl.BlockSpec((1, tm, D_MODEL), lambda i, j: (i, j, 0))
    return pl.pallas_call(
        functools.partial(_post_kernel, final=final, ff_block=1024),
        grid=(b, nt),
        in_specs=[xs, tok, tok, tok, tok, tok,
                  pl.BlockSpec((1, 1, 8, MIXER_W), lambda i, j: (i, j, 0, 0)),
                  _const_spec((3, MIXER_W)), _const_spec((D_MODEL, D_MODEL)),
                  _const_spec((1, D_MODEL)), _const_spec((D_MODEL, D_FF)), _const_spec((D_FF, D_MODEL)),
                  _const_spec((1, D_MODEL)), _const_spec((D_MODEL, D_MODEL)),
                  pl.BlockSpec((1, 1, tm, PLE_DIM), lambda i, j: (layer, i, j, 0)),
                  _const_spec((PLE_DIM, D_MODEL)), _const_spec((1, D_MODEL))],
        out_specs=xs,
        out_shape=jax.ShapeDtypeStruct((b, s, D_MODEL), F32),
        compiler_params=_params("parallel", "parallel"),
    )(x, y_ret, y_fft, y_att, cb, z, halo, conv_w, w_out, nf, w1, w2, npl, wg, p, wp, fn)


def _rope_tables(s):
    t = jnp.arange(s, dtype=F32)
    half = HEAD_DIM // 2
    inv = ROPE_BASE ** (-jnp.arange(half, dtype=F32) / half)
    ang = t[:, None] * inv[None, :]
    cos, sin = jnp.cos(ang), jnp.sin(ang)
    rcos = jnp.tile(jnp.concatenate([cos, cos], axis=-1), (1, RET_HEADS))
    rsin = jnp.tile(jnp.concatenate([-sin, sin], axis=-1), (1, RET_HEADS))
    quarter = half // 2
    inv_a = ROPE_BASE ** (-jnp.arange(quarter, dtype=F32) / quarter)
    ang_r = (jnp.arange(s) // GRID_W).astype(F32)[:, None] * inv_a[None, :]
    ang_c = (jnp.arange(s) % GRID_W).astype(F32)[:, None] * inv_a[None, :]
    cr, sr, cc, sc = jnp.cos(ang_r), jnp.sin(ang_r), jnp.cos(ang_c), jnp.sin(ang_c)
    acos = jnp.tile(jnp.concatenate([cr, cr, cc, cc], axis=-1), (1, ATT_Q_HEADS))
    asin = jnp.tile(jnp.concatenate([-sr, sr, -sc, sc], axis=-1), (1, ATT_Q_HEADS))
    return rcos, rsin, acos, asin


def _encode(x, p, wts, depth):
    b, s, _ = x.shape
    tabs = _rope_tables(s)
    hd = np.arange(MIXER_W) // HEAD_DIM
    gmean = jnp.asarray((hd[:, None] == hd[None, :]) / HEAD_DIM, F32).astype(BF16)
    for l in range(depth):
        w = wts[l]
        rq, rkt, rv, rg, fu, aq, akt, av, cb, z = _in_proj(
            x, w["norm_mix"], w["w_in"], tabs, w["q_norm"], w["k_norm"], gmean, tm=1024)
        y_ret = _retention(rq, rkt, rv, rg, w["ret_tabs"], gmean, w["ret_gn"])
        y_fft = _fourier(fu)
        y_att = _attention(aq, akt, av, tq=1024, kv_block=256)
        x = _post(x, y_ret, y_fft, y_att, cb, z, w["conv_w"], w["w_out"], w["norm_ffn"], w["w_ffn1"],
                  w["w_ffn2"], w["norm_pl"], w["w_pl_gate"], p, l, w["w_pl_proj"], w["final_norm"],
                  tm=512, final=(l == depth - 1))
    return x


def kernel(x_prompt, x_sample, p_prompt, p_sample, norm_mix_w, w_in, ret_log_rate, ret_gn_w, q_norm_w, k_norm_w,
           conv_w, w_out, norm_ffn_w, w_ffn1, w_ffn2, norm_pl_w, w_pl_gate, w_pl_proj, final_norm_w):
    depth = w_in.shape[0]
    wts = []
    for l in range(depth):
        wts.append(dict(
            norm_mix=norm_mix_w[l][None, :], w_in=w_in[l].astype(BF16),
            ret_tabs=_retention_tables(ret_log_rate[l]), ret_gn=ret_gn_w[l][None, :],
            q_norm=jnp.tile(q_norm_w[l], ATT_Q_HEADS)[None, :], k_norm=jnp.tile(k_norm_w[l], ATT_KV_HEADS)[None, :],
            conv_w=conv_w[l], w_out=w_out[l].astype(BF16), norm_ffn=norm_ffn_w[l][None, :],
            w_ffn1=w_ffn1[l].astype(BF16), w_ffn2=w_ffn2[l].astype(BF16), norm_pl=norm_pl_w[l][None, :],
            w_pl_gate=w_pl_gate[l].astype(BF16), w_pl_proj=w_pl_proj[l].astype(BF16),
            final_norm=final_norm_w[None, :]))
    return _encode(x_prompt, p_prompt, wts, depth), _encode(x_sample, p_sample, wts, depth)
```
